```python
import math
import jax
import jax.numpy as jnp
from jax import lax
import numpy as np

D_MODEL = 4096
BATCH = 2
SEQ = 8192
DEPTH = 2

N_A_LAYERS = DEPTH // 2
N_B_LAYERS = DEPTH - N_A_LAYERS
CHUNK = 128
A_WIDTH = D_MODEL
A_GROUPS = 8
A_GROUP_DIM = A_WIDTH // A_GROUPS
HEAD_DIM = 128
N_HEADS = D_MODEL // (2 * HEAD_DIM)
ATT_WIDTH = 2 * N_HEADS * HEAD_DIM
Q_BLOCK = 128
N_EXPERTS = 32
TOP_K = 4
EXPERT_FF = D_MODEL // 8
SWIGLU_ALPHA = 1.702
SWIGLU_LIMIT = 7.0
MOE_BLOCK = 128
RMS_EPS = 1e-6
LN_EPS = 1e-5

kernel_name = 'yoco_gmlp_diffattn_moe_adaln'


def _rms_norm(x, g):
    xf = x.astype(jnp.float32)
    y = xf * lax.rsqrt(jnp.mean(xf * xf, axis=-1, keepdims=True) + RMS_EPS)
    return (y * g.astype(jnp.float32)).astype(x.dtype)


def _layer_norm(x, g, b):
    xf = x.astype(jnp.float32)
    mu = jnp.mean(xf, axis=-1, keepdims=True)
    var = jnp.mean(jnp.square(xf - mu), axis=-1, keepdims=True)
    y = (xf - mu) * lax.rsqrt(var + LN_EPS)
    return (y * g.astype(jnp.float32) + b.astype(jnp.float32)).astype(x.dtype)


def _modulation(c, w, b, n):
    m = jax.nn.silu(c) @ w + b
    return [t[:, None, :] for t in jnp.split(m, n, axis=-1)]


def _modulated_norm(h, g, shift, scale):
    return _rms_norm(h, g) * (1.0 + scale) + shift


def _chunk_gmlp(n, w_in, ln_g, ln_b, w_s, b_s, w_out):
    b, s, _ = n.shape
    z = jax.nn.gelu(n @ w_in, approximate=False)
    u, v = jnp.split(z, 2, axis=-1)
    v = _layer_norm(v, ln_g, ln_b).reshape(b, s // CHUNK, CHUNK, A_GROUPS, A_GROUP_DIM)
    causal = jnp.tril(jnp.ones((CHUNK, CHUNK), dtype=bool))
    ws = jnp.where(causal[None], w_s, jnp.zeros_like(w_s))
    sv = jnp.einsum('gts,bcsgh->bctgh', ws, v) + b_s.T[:, :, None]
    return (u * sv.reshape(b, s, A_WIDTH)) @ w_out


def _shared_kv(h, c, kv_g, kv_mod_w, kv_mod_b, w_k, w_v):
    b, s, _ = h.shape
    shift, scale = _modulation(c, kv_mod_w, kv_mod_b, 2)
    n = _modulated_norm(h, kv_g, shift, scale)
    k = (n @ w_k).reshape(b, s, N_HEADS, 2, HEAD_DIM)
    v = (n @ w_v).reshape(b, s, N_HEADS, 2 * HEAD_DIM)
    return k, v


def _lambda_init(layer):
    return 0.8 - 0.6 * math.exp(-0.3 * layer)


def _diff_attention(n, k, v, w_q, lq1, lk1, lq2, lk2, subln_g, w_o, lambda_init):
    b, s, _ = n.shape
    f32 = jnp.float32
    q = (n @ w_q).reshape(b, s, N_HEADS, 2, HEAD_DIM) * (HEAD_DIM ** -0.5)
    lam = (jnp.exp(jnp.sum(lq1.astype(f32) * lk1.astype(f32)))
           - jnp.exp(jnp.sum(lq2.astype(f32) * lk2.astype(f32))) + lambda_init)
    key_pos = jnp.arange(s)

    def q_block(i):
        start = i * Q_BLOCK
        qb = lax.dynamic_slice_in_dim(q, start, Q_BLOCK, axis=1)
        scores = jnp.einsum('bqhnd,bkhnd->bhnqk', qb, k, preferred_element_type=f32)
        q_pos = start + jnp.arange(Q_BLOCK)
        mask = key_pos[None, :] <= q_pos[:, None]
        probs = jax.nn.softmax(jnp.where(mask, scores, -jnp.inf), axis=-1)
        attn = probs[:, :, 0] - lam * probs[:, :, 1]
        return jnp.einsum('bhqk,bkhe->bqhe', attn.astype(v.dtype), v)

    o = lax.map(q_block, jnp.arange(s // Q_BLOCK))
    o = jnp.transpose(o, (1, 0, 2, 3, 4)).reshape(b, s, N_HEADS, 2 * HEAD_DIM)
    o = _rms_norm(o, subln_g) * (1.0 - lambda_init)
    return o.reshape(b, s, ATT_WIDTH) @ w_o


def _clamped_swiglu(h):
    glu, lin = h[..., ::2], h[..., 1::2]
    glu = jnp.minimum(glu, SWIGLU_LIMIT)
    lin = jnp.clip(lin, -SWIGLU_LIMIT, SWIGLU_LIMIT)
    return glu * jax.nn.sigmoid(SWIGLU_ALPHA * glu) * (lin + 1.0)


def _moe(n, w_r, b_r, w1, b1, w2, b2):
    b, s, d = n.shape
    n_tok = b * s
    n_assign = n_tok * TOP_K
    xf = n.reshape(n_tok, d)
    logits = jnp.dot(xf, w_r, preferred_element_type=jnp.float32) + b_r.astype(jnp.float32)
    top_v, top_i = lax.top_k(logits, TOP_K)
    gates = jax.nn.softmax(top_v, axis=-1)
    e_flat = top_i.reshape(-1)
    t_flat = jnp.arange(n_assign, dtype=jnp.int32) // TOP_K
    order = jnp.argsort(e_flat)
    e_sorted = e_flat[order]
    counts = jnp.bincount(e_flat, length=N_EXPERTS)
    padded = (counts + MOE_BLOCK - 1) // MOE_BLOCK * MOE_BLOCK
    starts = jnp.cumsum(counts) - counts
    p_ends = jnp.cumsum(padded)
    p_starts = p_ends - padded
    dest = p_starts[e_sorted] + jnp.arange(n_assign, dtype=jnp.int32) - starts[e_sorted]
    n_blocks = -(-n_assign // MOE_BLOCK) + N_EXPERTS
    n_rows = n_blocks * MOE_BLOCK
    row_tok = jnp.zeros((n_rows,), jnp.int32).at[dest].set(t_flat[order])
    row_gate = jnp.zeros((n_rows,), jnp.float32).at[dest].set(gates.reshape(-1)[order])
    blk_exp = jnp.minimum(
        jnp.searchsorted(p_ends, jnp.arange(n_blocks) * MOE_BLOCK, side='right'), N_EXPERTS - 1)

    def expert_block(args):
        tok, e = args
        hb = xf[tok] @ w1[e] + b1[e]
        return _clamped_swiglu(hb) @ w2[e] + b2[e]

    yb = lax.map(expert_block, (row_tok.reshape(n_blocks, MOE_BLOCK), blk_exp))
    yb = yb.reshape(n_rows, d) * row_gate[:, None].astype(yb.dtype)
    y = jax.ops.segment_sum(yb, row_tok, num_segments=n_tok)
    return y.reshape(b, s, d)


def setup_inputs(seed: int = 0) -> dict:
    key = jax.random.key(seed)
    ks = iter(jax.random.split(key, 48))
    D = D_MODEL

    def nrm(shape, scale):
        return jax.random.normal(next(ks), shape, jnp.float32) * scale

    def gain(shape):
        return 1.0 + nrm(shape, 0.05)

    return {
        'x': nrm((BATCH, SEQ, D), 1.0),
        'c': nrm((BATCH, D), 1.0),
        'mix_pre_g': gain((DEPTH, D)),
        'mix_post_g': gain((DEPTH, D)),
        'mix_mod_w': nrm((DEPTH, D, 3 * D), 0.5 * D ** -0.5),
        'mix_mod_b': nrm((DEPTH, 3 * D), 0.01),
        'ffn_pre_g': gain((DEPTH, D)),
        'ffn_post_g': gain((DEPTH, D)),
        'ffn_mod_w': nrm((DEPTH, D, 3 * D), 0.5 * D ** -0.5),
        'ffn_mod_b': nrm((DEPTH, 3 * D), 0.01),
        'a_w_in': nrm((N_A_LAYERS, D, 2 * A_WIDTH), D ** -0.5),
        'a_ln_g': gain((N_A_LAYERS, A_WIDTH)),
        'a_ln_b': nrm((N_A_LAYERS, A_WIDTH), 0.01),
        'a_w_s': nrm((N_A_LAYERS, A_GROUPS, CHUNK, CHUNK), 0.5 * CHUNK ** -0.5),
        'a_b_s': gain((N_A_LAYERS, A_GROUPS, CHUNK)),
        'a_w_out': nrm((N_A_LAYERS, A_WIDTH, D), A_WIDTH ** -0.5),
        'kv_g': gain((D,)),
        'kv_mod_w': nrm((D, 2 * D), 0.5 * D ** -0.5),
        'kv_mod_b': nrm((2 * D,), 0.01),
        'w_k': nrm((D, ATT_WIDTH), D ** -0.5),
        'w_v': nrm((D, ATT_WIDTH), D ** -0.5),
        'b_w_q': nrm((N_B_LAYERS, D, ATT_WIDTH), D ** -0.5),
        'b_lq1': nrm((N_B_LAYERS, HEAD_DIM), 0.1),
        'b_lk1': nrm((N_B_LAYERS, HEAD_DIM), 0.1),
        'b_lq2': nrm((N_B_LAYERS, HEAD_DIM), 0.1),
        'b_lk2': nrm((N_B_LAYERS, HEAD_DIM), 0.1),
        'b_subln_g': gain((N_B_LAYERS, 2 * HEAD_DIM)),
        'b_w_o': nrm((N_B_LAYERS, ATT_WIDTH, D), ATT_WIDTH ** -0.5),
        'moe_w_r': nrm((DEPTH, D, N_EXPERTS), D ** -0.5),
        'moe_b_r': nrm((DEPTH, N_EXPERTS), 0.01),
        'moe_w1': nrm((DEPTH, N_EXPERTS, D, 2 * EXPERT_FF), D ** -0.5),
        'moe_b1': nrm((DEPTH, N_EXPERTS, 2 * EXPERT_FF), 0.01),
        'moe_w2': nrm((DEPTH, N_EXPERTS, EXPERT_FF, D), EXPERT_FF ** -0.5),
        'moe_b2': nrm((DEPTH, N_EXPERTS, D), 0.01),
    }


def reference(x, c, mix_pre_g, mix_post_g, mix_mod_w, mix_mod_b, ffn_pre_g, ffn_post_g,
              ffn_mod_w, ffn_mod_b, a_w_in, a_ln_g, a_ln_b, a_w_s, a_b_s, a_w_out,
              kv_g, kv_mod_w, kv_mod_b, w_k, w_v, b_w_q, b_lq1, b_lk1, b_lq2, b_lk2,
              b_subln_g, b_w_o, moe_w_r, moe_b_r, moe_w1, moe_b1, moe_w2, moe_b2):
    h = x
    k_sh = None
    v_sh = None
    for l in range(DEPTH):
        shift, scale, gate = _modulation(c, mix_mod_w[l], mix_mod_b[l], 3)
        n = _modulated_norm(h, mix_pre_g[l], shift, scale)
        if l < N_A_LAYERS:
            out = _chunk_gmlp(n, a_w_in[l], a_ln_g[l], a_ln_b[l], a_w_s[l], a_b_s[l], a_w_out[l])
        else:
            if l == N_A_LAYERS:
                k_sh, v_sh = _shared_kv(h, c, kv_g, kv_mod_w, kv_mod_b, w_k, w_v)
            j = l - N_A_LAYERS
            out = _diff_attention(n, k_sh, v_sh, b_w_q[j], b_lq1[j], b_lk1[j], b_lq2[j], b_lk2[j],
                                  b_subln_g[j], b_w_o[j], _lambda_init(l))
        h = h + gate * _rms_norm(out, mix_post_g[l])
        shift, scale, gate = _modulation(c, ffn_mod_w[l], ffn_mod_b[l], 3)
        n = _modulated_norm(h, ffn_pre_g[l], shift, scale)
        out = _moe(n, moe_w_r[l], moe_b_r[l], moe_w1[l], moe_b1[l], moe_w2[l], moe_b2[l])
        h = h + gate * _rms_norm(out, ffn_post_g[l])
    return h
```

```python
import functools
import math

import jax
import jax.numpy as jnp
from jax import lax
from jax.experimental import pallas as pl
from jax.experimental.pallas import tpu as pltpu

RMS_EPS = 1e-6
LN_EPS = 1e-5
TOP_K = 4
SWIGLU_ALPHA = 1.702
SWIGLU_LIMIT = 7.0
LOG2E = 1.4426950408889634

VMEM_LIMIT_BYTES = 56 * 1024 * 1024
LANES = 128
MOE_TILE = 256

BF16 = jnp.bfloat16
F32 = jnp.float32


def _params(*sem):
    return pltpu.CompilerParams(dimension_semantics=sem, vmem_limit_bytes=VMEM_LIMIT_BYTES)


def _rms(x):
    return x * lax.rsqrt(jnp.mean(x * x, axis=-1, keepdims=True) + RMS_EPS)


def _mod_kernel(c_ref, w_ref, b_ref, o_ref):
    c = c_ref[...]
    s = c * jax.nn.sigmoid(c)
    s_hi = s.astype(BF16)
    s_lo = (s - s_hi.astype(F32)).astype(BF16)
    lhs = jnp.concatenate([s_hi, s_lo], axis=0)
    r = jnp.dot(lhs, w_ref[...].astype(BF16), preferred_element_type=F32)
    rows = c.shape[0]
    o_ref[...] = r[:rows] + r[rows:] + b_ref[...]


def _modulation(c_pad, w3, b2, layer, n_split, batch):
    rows, d = c_pad.shape
    nout = w3.shape[-1]
    tn = min(512, nout)
    out = pl.pallas_call(
        _mod_kernel,
        grid=(nout // tn,),
        in_specs=[
            pl.BlockSpec((rows, d), lambda j: (0, 0)),
            pl.BlockSpec((None, d, tn), lambda j: (layer, 0, j)),
            pl.BlockSpec((None, 1, tn), lambda j: (layer, 0, j)),
        ],
        out_specs=pl.BlockSpec((rows, tn), lambda j: (0, j)),
        out_shape=jax.ShapeDtypeStruct((rows, nout), F32),
        compiler_params=_params("arbitrary"),
        name="modulation",
    )(c_pad, w3, b2.reshape(b2.shape[0], 1, nout))
    return [t[:batch, None, :] for t in jnp.split(out, n_split, axis=-1)]


def _modnorm_kernel(n_out, h_ref, *refs):
    y = _rms(h_ref[...])
    for t in range(n_out):
        g_ref, sh_ref, sc_ref = refs[3 * t:3 * t + 3]
        o_ref = refs[3 * n_out + t]
        o_ref[...] = ((y * g_ref[...]) * (1.0 + sc_ref[...]) + sh_ref[...]).astype(o_ref.dtype)


def _modnorm(h, mods, seq):
    n, d = h.shape
    tm = min(512, seq)
    per_b = seq // tm
    ins, specs = [h], [pl.BlockSpec((tm, d), lambda i: (i, 0))]
    for g, sh, sc in mods:
        ins += [g.reshape(1, d), sh, sc]
        specs += [pl.BlockSpec((1, d), lambda i: (0, 0)),
                  pl.BlockSpec((None, 1, d), lambda i: (i // per_b, 0, 0)),
                  pl.BlockSpec((None, 1, d), lambda i: (i // per_b, 0, 0))]
    outs = pl.pallas_call(
        functools.partial(_modnorm_kernel, len(mods)),
        grid=(n // tm,),
        in_specs=specs,
        out_specs=[pl.BlockSpec((tm, d), lambda i: (i, 0)) for _ in mods],
        out_shape=[jax.ShapeDtypeStruct((n, d), BF16) for _ in mods],
        compiler_params=_params("parallel"),
        name="modnorm",
    )(*ins)
    return list(outs)


def _matmul_kernel(epilogue, scale, x_ref, w_ref, o_ref):
    acc = jnp.dot(x_ref[...], w_ref[...], preferred_element_type=F32)
    if epilogue == "gelu":
        acc = 0.5 * acc * (1.0 + lax.erf(acc * (1.0 / math.sqrt(2.0))))
    elif epilogue == "scale":
        acc = acc * scale
    o_ref[...] = acc.astype(o_ref.dtype)


def _matmul(x, w, out_dtype, epilogue=None, scale=1.0, name="matmul"):
    m, k = x.shape
    nn = w.shape[1]
    tm = min(1024, m)
    tn = min(1024 if out_dtype == BF16 else 512, nn)
    return pl.pallas_call(
        functools.partial(_matmul_kernel, epilogue, scale),
        grid=(m // tm, nn // tn),
        in_specs=[pl.BlockSpec((tm, k), lambda i, j: (i, 0)),
                  pl.BlockSpec((k, tn), lambda i, j: (0, j))],
        out_specs=pl.BlockSpec((tm, tn), lambda i, j: (i, j)),
        out_shape=jax.ShapeDtypeStruct((m, nn), out_dtype),
        compiler_params=_params("parallel", "arbitrary"),
        name=name,
    )(x, w)


def _gate_kernel(chunk, groups, u_ref, v_ref, g_ref, b_ref, ws_ref, bs_ref, o_ref):
    v = v_ref[...].astype(F32)
    mu = jnp.mean(v, axis=-1, keepdims=True)
    vc = v - mu
    var = jnp.mean(vc * vc, axis=-1, keepdims=True)
    vln = (vc * lax.rsqrt(var + LN_EPS) * g_ref[...] + b_ref[...]).astype(BF16)
    tm, width = vln.shape
    gd = width // groups
    causal = (lax.broadcasted_iota(jnp.int32, (chunk, chunk), 0)
              >= lax.broadcasted_iota(jnp.int32, (chunk, chunk), 1))
    for g in range(groups):
        ws = jnp.where(causal, ws_ref[g], 0.0).astype(BF16)
        bias = bs_ref[:, g:g + 1]
        for c in range(tm // chunk):
            rows = slice(c * chunk, (c + 1) * chunk)
            cols = slice(g * gd, (g + 1) * gd)
            sv = jnp.dot(ws, vln[rows, cols], preferred_element_type=F32) + bias
            o_ref[rows, cols] = (u_ref[rows, cols].astype(F32) * sv).astype(o_ref.dtype)


def _gmlp_gate(z, ln_g, ln_b, w_s, b_s):
    n, w2 = z.shape
    width = w2 // 2
    groups, chunk, _ = w_s.shape
    tm = min(2 * chunk, n)
    return pl.pallas_call(
        functools.partial(_gate_kernel, chunk, groups),
        grid=(n // tm,),
        in_specs=[pl.BlockSpec((tm, width), lambda i: (i, 0)),
                  pl.BlockSpec((tm, width), lambda i: (i, 1)),
                  pl.BlockSpec((1, width), lambda i: (0, 0)),
                  pl.BlockSpec((1, width), lambda i: (0, 0)),
                  pl.BlockSpec((groups, chunk, chunk), lambda i: (0, 0, 0)),
                  pl.BlockSpec((chunk, groups), lambda i: (0, 0))],
        out_specs=pl.BlockSpec((tm, width), lambda i: (i, 0)),
        out_shape=jax.ShapeDtypeStruct((n, width), BF16),
        compiler_params=_params("parallel"),
        name="gmlp_gate",
    )(z, z, ln_g.reshape(1, width), ln_b.reshape(1, width), w_s, b_s.T)


def _post_kernel(h_ref, y_ref, g_ref, gate_ref, o_ref):
    o_ref[...] = h_ref[...] + gate_ref[...] * (_rms(y_ref[...]) * g_ref[...])


def _post(h, y, g, gate, seq):
    n, d = h.shape
    tm = min(256, seq)
    per_b = seq // tm
    return pl.pallas_call(
        _post_kernel,
        grid=(n // tm,),
        in_specs=[pl.BlockSpec((tm, d), lambda i: (i, 0)),
                  pl.BlockSpec((tm, d), lambda i: (i, 0)),
                  pl.BlockSpec((1, d), lambda i: (0, 0)),
                  pl.BlockSpec((None, 1, d), lambda i: (i // per_b, 0, 0))],
        out_specs=pl.BlockSpec((tm, d), lambda i: (i, 0)),
        out_shape=jax.ShapeDtypeStruct((n, d), F32),
        compiler_params=_params("parallel"),
        name="post_residual",
    )(h, y, g.reshape(1, d), gate)


def _attn_kernel(tq, tk, dh, lambda_init, q_ref, k_ref, v_ref, lq1_ref, lk1_ref, lq2_ref, lk2_ref,
                 sg_ref, o_ref):
    qi = pl.program_id(2)
    q = q_ref[...]
    q1, q2 = q[:, :dh], q[:, dh:]
    dims = (((1,), (1,)), ((), ()))

    def step(ki, carry, masked):
        m1, l1, a1, m2, l2, a2 = carry
        start = pl.multiple_of(ki * tk, tk)
        k1 = k_ref[pl.ds(start, tk), pl.ds(0, dh)]
        k2 = k_ref[pl.ds(start, tk), pl.ds(dh, dh)]
        v = v_ref[pl.ds(start, tk), :]
        s1 = lax.dot_general(q1, k1, dims, preferred_element_type=F32)
        s2 = lax.dot_general(q2, k2, dims, preferred_element_type=F32)
        if masked:
            row = qi * tq + lax.broadcasted_iota(jnp.int32, (tq, tk), 0)
            col = ki * tk + lax.broadcasted_iota(jnp.int32, (tq, tk), 1)
            keep = col <= row
            s1 = jnp.where(keep, s1, -jnp.inf)
            s2 = jnp.where(keep, s2, -jnp.inf)
        out = []
        for s, m, l, a in ((s1, m1, l1, a1), (s2, m2, l2, a2)):
            mn = jnp.maximum(m, jnp.max(s, axis=-1, keepdims=True))
            alpha = jnp.exp2(m - mn)
            p = jnp.exp2(s - mn)
            l = alpha * l + jnp.sum(p, axis=-1, keepdims=True)
            a = alpha * a + jnp.dot(p.astype(BF16), v, preferred_element_type=F32)
            out += [mn, l, a]
        return tuple(out)

    neg = jnp.full((tq, 1), -jnp.inf, F32)
    zero = jnp.zeros((tq, 1), F32)
    acc0 = jnp.zeros((tq, 2 * dh), F32)
    carry = (neg, zero, acc0, neg, zero, acc0)
    n_full = (qi * tq) // tk
    carry = lax.fori_loop(0, n_full, lambda ki, cr: step(ki, cr, False), carry)
    for d in range(tq // tk):
        carry = step(n_full + d, carry, True)
    _, l1, a1, _, l2, a2 = carry

    lam = (jnp.exp(jnp.sum(lq1_ref[...] * lk1_ref[...], axis=-1, keepdims=True))
           - jnp.exp(jnp.sum(lq2_ref[...] * lk2_ref[...], axis=-1, keepdims=True)) + lambda_init)
    o = a1 / l1 - lam * (a2 / l2)
    o = _rms(o) * sg_ref[...] * (1.0 - lambda_init)
    o_ref[...] = o.astype(o_ref.dtype)


def _diff_attention(q, kv, lq1, lk1, lq2, lk2, subln_g, lambda_init):
    b, s, aw = q.shape
    dh = lq1.shape[-1]
    heads = aw // (2 * dh)
    tq = min(1024, s)
    tk = min(512, s)
    vec = lambda a: a.reshape(1, -1)
    small = lambda w: pl.BlockSpec((1, w), lambda bi, hi, qi: (0, 0))
    return pl.pallas_call(
        functools.partial(_attn_kernel, tq, tk, dh, lambda_init),
        grid=(b, heads, s // tq),
        in_specs=[pl.BlockSpec((None, tq, 2 * dh), lambda bi, hi, qi: (bi, qi, hi)),
                  pl.BlockSpec((None, s, 2 * dh), lambda bi, hi, qi: (bi, 0, hi)),
                  pl.BlockSpec((None, s, 2 * dh), lambda bi, hi, qi: (bi, 0, heads + hi)),
                  small(dh), small(dh), small(dh), small(dh), small(2 * dh)],
        out_specs=pl.BlockSpec((None, tq, 2 * dh), lambda bi, hi, qi: (bi, qi, hi)),
        out_shape=jax.ShapeDtypeStruct((b, s, aw), BF16),
        compiler_params=_params("parallel", "parallel", "arbitrary"),
        name="diff_attention",
    )(q, kv, kv, vec(lq1), vec(lk1), vec(lq2), vec(lk2), vec(subln_g))


def _router_kernel(h_ref, g_ref, sh_ref, sc_ref, wr_ref, br_ref,
                   np_ref, idx_ref, gate_ref, rank_ref, cnt_ref, carry_ref):
    i = pl.program_id(0)

    @pl.when(i == 0)
    def _():
        carry_ref[...] = jnp.zeros_like(carry_ref)

    n = (_rms(h_ref[...]) * g_ref[...]) * (1.0 + sc_ref[...]) + sh_ref[...]
    tm, d = n.shape
    n_hi = n.astype(BF16)
    n_hi32 = n_hi.astype(F32)
    n_lo = (n - n_hi32).astype(BF16)

    bits = pltpu.bitcast(n_hi32, jnp.uint32)
    half = d // 2
    np_ref[...] = (lax.shift_right_logical(bits[:, :half], jnp.uint32(16))
                   | (bits[:, half:] & jnp.uint32(0xFFFF0000)))

    wr = wr_ref[...]
    w_hi = wr.astype(BF16)
    w_lo = (wr - w_hi.astype(F32)).astype(BF16)
    dims = (((1,), (1,)), ((), ()))
    logits = (lax.dot_general(w_hi, n_hi, dims, preferred_element_type=F32)
              + lax.dot_general(w_lo, n_hi, dims, preferred_element_type=F32)
              + lax.dot_general(w_hi, n_lo, dims, preferred_element_type=F32)
              + br_ref[...])
    n_exp = logits.shape[0]
    eid = lax.broadcasted_iota(jnp.int32, (n_exp, tm), 0).astype(F32)

    vals, sels, ids = [], [], []
    rem = logits
    for _ in range(TOP_K):
        m = jnp.max(rem, axis=0, keepdims=True)
        first = jnp.min(jnp.where(rem == m, eid, float(n_exp)), axis=0, keepdims=True)
        sel = eid == first
        rem = jnp.where(sel, -jnp.inf, rem)
        vals.append(m)
        sels.append(sel)
        ids.append(first)

    exps = [jnp.exp(v - vals[0]) for v in vals]
    denom = exps[0]
    for e in exps[1:]:
        denom = denom + e

    chosen = sels[0]
    for sel in sels[1:]:
        chosen = jnp.logical_or(chosen, sel)
    chosen = jnp.where(chosen, 1.0, 0.0)
    before = (lax.broadcasted_iota(jnp.int32, (tm, tm), 0)
              < lax.broadcasted_iota(jnp.int32, (tm, tm), 1))
    upper = jnp.where(before, 1.0, 0.0).astype(BF16)
    rank_excl = jnp.dot(chosen.astype(BF16), upper, preferred_element_type=F32) + carry_ref[...]

    for k in range(TOP_K):
        idx_ref[k:k + 1, :] = ids[k].astype(jnp.int32)
        gate_ref[k:k + 1, :] = exps[k] / denom
        rank_ref[k:k + 1, :] = jnp.sum(jnp.where(sels[k], rank_excl, 0.0), axis=0,
                                       keepdims=True).astype(jnp.int32)

    total = carry_ref[...] + jnp.sum(chosen, axis=1, keepdims=True)
    carry_ref[...] = total
    cnt_ref[...] = jnp.broadcast_to(total, cnt_ref.shape).astype(jnp.int32)


def _router(h, g, shift, scale, w_r, b_r, seq):
    n, d = h.shape
    n_exp = w_r.shape[1]
    tm = min(512, seq)
    per_b = seq // tm
    tok = lambda dt: jax.ShapeDtypeStruct((TOP_K, n), dt)
    tok_spec = pl.BlockSpec((TOP_K, tm), lambda i: (0, i))
    return pl.pallas_call(
        _router_kernel,
        grid=(n // tm,),
        in_specs=[pl.BlockSpec((tm, d), lambda i: (i, 0)),
                  pl.BlockSpec((1, d), lambda i: (0, 0)),
                  pl.BlockSpec((None, 1, d), lambda i: (i // per_b, 0, 0)),
                  pl.BlockSpec((None, 1, d), lambda i: (i // per_b, 0, 0)),
                  pl.BlockSpec((n_exp, d), lambda i: (0, 0)),
                  pl.BlockSpec((n_exp, 1), lambda i: (0, 0))],
        out_specs=[pl.BlockSpec((tm, d // 2), lambda i: (i, 0)), tok_spec, tok_spec, tok_spec,
                   pl.BlockSpec((n_exp, LANES), lambda i: (0, 0))],
        out_shape=[jax.ShapeDtypeStruct((n, d // 2), jnp.uint32), tok(jnp.int32), tok(F32),
                   tok(jnp.int32), jax.ShapeDtypeStruct((n_exp, LANES), jnp.int32)],
        scratch_shapes=[pltpu.VMEM((n_exp, 1), F32)],
        compiler_params=_params("arbitrary"),
        name="moe_router",
    )(h, g.reshape(1, d), shift, scale, w_r.T, b_r.reshape(n_exp, 1))


def _dispatch_kernel(n_tok, tm, dest_ref, np_ref, xs_ref, sem):
    i = pl.program_id(0)

    def row_copy(r, k):
        d = dest_ref[k * n_tok + i * tm + r]
        return pltpu.make_async_copy(np_ref.at[pl.ds(r, 1)], xs_ref.at[pl.ds(d, 1)], sem)

    def issue(r, _):
        for k in range(TOP_K):
            row_copy(r, k).start()
        return 0

    lax.fori_loop(0, tm, issue, 0)

    def drain(r, _):
        for k in range(TOP_K):
            row_copy(r, k).wait()
        return 0

    lax.fori_loop(0, tm, drain, 0)


def _dispatch(n_packed, dest_flat, n_rows):
    n, half = n_packed.shape
    tm = min(256, n)
    return pl.pallas_call(
        functools.partial(_dispatch_kernel, n, tm),
        grid_spec=pltpu.PrefetchScalarGridSpec(
            num_scalar_prefetch=1,
            grid=(n // tm,),
            in_specs=[pl.BlockSpec((tm, half), lambda i, dest: (i, 0))],
            out_specs=pl.BlockSpec(memory_space=pl.ANY),
            scratch_shapes=[pltpu.SemaphoreType.DMA(())]),
        out_shape=jax.ShapeDtypeStruct((n_rows, half), jnp.uint32),
        compiler_params=_params("arbitrary"),
        name="moe_dispatch",
    )(dest_flat, n_packed)


def _expert_kernel(texp_ref, nused_ref, x_ref, w1g_ref, w1l_ref, w2_ref, b1g_ref, b1l_ref, b2_ref, o_ref):
    i = pl.program_id(0)

    @pl.when(i < nused_ref[0])
    def _():
        w = x_ref[...]
        half = w.shape[1]
        x_lo = pltpu.bitcast(lax.shift_left(w, jnp.uint32(16)), F32).astype(BF16)
        x_hi = pltpu.bitcast(w & jnp.uint32(0xFFFF0000), F32).astype(BF16)

        def up(w_ref, b_ref):
            return (jnp.dot(x_lo, w_ref[:half, :], preferred_element_type=F32)
                    + jnp.dot(x_hi, w_ref[half:, :], preferred_element_type=F32) + b_ref[...])

        glu = jnp.minimum(up(w1g_ref, b1g_ref), SWIGLU_LIMIT)
        lin = jnp.clip(up(w1l_ref, b1l_ref), -SWIGLU_LIMIT, SWIGLU_LIMIT)
        act = glu * jax.nn.sigmoid(SWIGLU_ALPHA * glu) * (lin + 1.0)
        o_ref[...] = jnp.dot(act.astype(BF16), w2_ref[...], preferred_element_type=F32) + b2_ref[...]


def _experts(xs, tile_exp, n_used, w1g, w1l, w2, b1g, b1l, b2):
    n_rows, half = xs.shape
    n_exp, d, ff = w1g.shape
    n_tiles = n_rows // MOE_TILE
    tile = lambda i, te, nu: (jnp.minimum(i, nu[0] - 1), 0)
    wmap = lambda i, te, nu: (te[jnp.minimum(i, nu[0] - 1)], 0, 0)
    return pl.pallas_call(
        _expert_kernel,
        grid_spec=pltpu.PrefetchScalarGridSpec(
            num_scalar_prefetch=2,
            grid=(n_tiles,),
            in_specs=[pl.BlockSpec((MOE_TILE, half), tile),
                      pl.BlockSpec((None, d, ff), wmap),
                      pl.BlockSpec((None, d, ff), wmap),
                      pl.BlockSpec((None, ff, d), wmap),
                      pl.BlockSpec((None, 1, ff), wmap),
                      pl.BlockSpec((None, 1, ff), wmap),
                      pl.BlockSpec((None, 1, d), wmap)],
            out_specs=pl.BlockSpec((MOE_TILE, d), tile)),
        out_shape=jax.ShapeDtypeStruct((n_rows, d), F32),
        compiler_params=_params("arbitrary"),
        name="moe_experts",
    )(tile_exp, n_used, xs, w1g, w1l, w2, b1g, b1l, b2)


def _combine_kernel(n_tok, tm, dest_ref, yb_ref, gt_ref, h_ref, g_ref, gate_ref, o_ref, buf, sems):
    i = pl.program_id(0)
    steps = pl.num_programs(0)

    def row_copy(step, slot, r, k):
        d = dest_ref[k * n_tok + step * tm + r]
        return pltpu.make_async_copy(yb_ref.at[pl.ds(d, 1)], buf.at[slot, k, pl.ds(r, 1)], sems.at[slot])

    def issue(step, slot):
        def body(r, _):
            for k in range(TOP_K):
                row_copy(step, slot, r, k).start()
            return 0
        lax.fori_loop(0, tm, body, 0)

    slot = i % 2

    @pl.when(i == 0)
    def _():
        issue(0, 0)

    @pl.when(i + 1 < steps)
    def _():
        issue(i + 1, 1 - slot)

    def drain(r, _):
        for k in range(TOP_K):
            row_copy(i, slot, r, k).wait()
        return 0

    lax.fori_loop(0, tm, drain, 0)

    y = buf[slot, 0] * gt_ref[:, 0:1]
    for k in range(1, TOP_K):
        y = y + buf[slot, k] * gt_ref[:, k:k + 1]
    o_ref[...] = h_ref[...] + gate_ref[...] * (_rms(y) * g_ref[...])


def _combine(yb, dest_flat, gates_t, h, g, gate, seq):
    n, d = h.shape
    tm = min(128, seq)
    per_b = seq // tm
    return pl.pallas_call(
        functools.partial(_combine_kernel, n, tm),
        grid_spec=pltpu.PrefetchScalarGridSpec(
            num_scalar_prefetch=1,
            grid=(n // tm,),
            in_specs=[pl.BlockSpec(memory_space=pl.ANY),
                      pl.BlockSpec((tm, TOP_K), lambda i, dest: (i, 0)),
                      pl.BlockSpec((tm, d), lambda i, dest: (i, 0)),
                      pl.BlockSpec((1, d), lambda i, dest: (0, 0)),
                      pl.BlockSpec((None, 1, d), lambda i, dest: (i // per_b, 0, 0))],
            out_specs=pl.BlockSpec((tm, d), lambda i, dest: (i, 0)),
            scratch_shapes=[pltpu.VMEM((2, TOP_K, tm, d), F32), pltpu.SemaphoreType.DMA((2,))]),
        out_shape=jax.ShapeDtypeStruct((n, d), F32),
        compiler_params=_params("arbitrary"),
        name="moe_combine",
    )(dest_flat, yb, gates_t, h, g.reshape(1, d), gate)


def _moe_layer(h, seq, pre_g, post_g, shift, scale, gate, w_r, b_r, w1, b1, w2, b2):
    n, d = h.shape
    n_exp = w_r.shape[1]
    n_packed, idx, gates, rank, cnt = _router(h, pre_g, shift, scale, w_r, b_r, seq)

    counts = cnt[:, 0]
    tiles_e = (counts + MOE_TILE - 1) // MOE_TILE
    tile_end = jnp.cumsum(tiles_e)
    row_start = (tile_end - tiles_e) * MOE_TILE
    dest = (row_start[idx] + rank).reshape(-1).astype(jnp.int32)
    n_tiles = (n * TOP_K) // MOE_TILE + n_exp
    tile_exp = jnp.minimum(jnp.searchsorted(tile_end, jnp.arange(n_tiles), side="right"),
                           n_exp - 1).astype(jnp.int32)
    n_used = tile_end[-1:].astype(jnp.int32)

    xs = _dispatch(n_packed, dest, n_tiles * MOE_TILE)
    ff = w2.shape[1]
    yb = _experts(xs, tile_exp, n_used,
                  w1[:, :, 0::2].astype(BF16), w1[:, :, 1::2].astype(BF16), w2.astype(BF16),
                  b1[:, 0::2].reshape(n_exp, 1, ff), b1[:, 1::2].reshape(n_exp, 1, ff),
                  b2.reshape(n_exp, 1, d))
    return _combine(yb, dest, gates.T, h, post_g, gate, seq)


def _lambda_init(layer):
    return 0.8 - 0.6 * math.exp(-0.3 * layer)


def kernel(x, c, mix_pre_g, mix_post_g, mix_mod_w, mix_mod_b, ffn_pre_g, ffn_post_g, ffn_mod_w, ffn_mod_b, a_w_in, a_ln_g, a_ln_b, a_w_s, a_b_s, a_w_out, kv_g, kv_mod_w, kv_mod_b, w_k, w_v, b_w_q, b_lq1, b_lk1, b_lq2, b_lk2, b_subln_g, b_w_o, moe_w_r, moe_b_r, moe_w1, moe_b1, moe_w2, moe_b2):
    batch, seq, d = x.shape
    depth = mix_pre_g.shape[0]
    n_a = a_w_in.shape[0]
    dh = b_lq1.shape[-1]
    h = x.reshape(batch * seq, d)
    c_pad = jnp.zeros((16, d), F32).at[:batch].set(c)

    kv = None
    for l in range(depth):
        shift, scale, gate = _modulation(c_pad, mix_mod_w, mix_mod_b, l, 3, batch)
        if l < n_a:
            (n,) = _modnorm(h, [(mix_pre_g[l], shift, scale)], seq)
            z = _matmul(n, a_w_in[l].astype(BF16), BF16, epilogue="gelu", name="gmlp_in")
            gated = _gmlp_gate(z, a_ln_g[l], a_ln_b[l], a_w_s[l], a_b_s[l])
            out = _matmul(gated, a_w_out[l].astype(BF16), F32, name="gmlp_out")
        else:
            j = l - n_a
            if kv is None:
                kv_shift, kv_scale = _modulation(c_pad, kv_mod_w[None], kv_mod_b[None], 0, 2, batch)
                n_kv, n = _modnorm(h, [(kv_g, kv_shift, kv_scale), (mix_pre_g[l], shift, scale)], seq)
                w_kv = jnp.concatenate([w_k, w_v], axis=1).astype(BF16)
                kv = _matmul(n_kv, w_kv, BF16, name="kv_proj").reshape(batch, seq, -1)
            else:
                (n,) = _modnorm(h, [(mix_pre_g[l], shift, scale)], seq)
            q = _matmul(n, b_w_q[j].astype(BF16), BF16, epilogue="scale",
                        scale=LOG2E * dh ** -0.5, name="q_proj").reshape(batch, seq, -1)
            o = _diff_attention(q, kv, b_lq1[j], b_lk1[j], b_lq2[j], b_lk2[j], b_subln_g[j],
                                _lambda_init(l))
            out = _matmul(o.reshape(batch * seq, -1), b_w_o[j].astype(BF16), F32, name="attn_out")
        h = _post(h, out, mix_post_g[l], gate, seq)

        shift, scale, gate = _modulation(c_pad, ffn_mod_w, ffn_mod_b, l, 3, batch)
        h = _moe_layer(h, seq, ffn_pre_g[l], ffn_post_g[l], shift, scale, gate,
                       moe_w_r[l], moe_b_r[l], moe_w1[l], moe_b1[l], moe_w2[l], moe_b2[l])
    return h.reshape(batch, seq, d)
```

```python
import functools
import math

import jax
import jax.numpy as jnp
from jax import lax
from jax.experimental import pallas as pl
from jax.experimental.pallas import tpu as pltpu

RMS_EPS = 1e-6
LN_EPS = 1e-5
TOP_K = 4
SWIGLU_ALPHA = 1.702
SWIGLU_LIMIT = 7.0
LOG2E = 1.4426950408889634

VMEM_LIMIT_BYTES = 56 * 1024 * 1024
LANES = 128
MOE_TILE = 256

BF16 = jnp.bfloat16
F32 = jnp.float32


def _params(*sem):
    return pltpu.CompilerParams(dimension_semantics=sem, vmem_limit_bytes=VMEM_LIMIT_BYTES)


def _rms(x):
    return x * lax.rsqrt(jnp.mean(x * x, axis=-1, keepdims=True) + RMS_EPS)


def _mod_kernel(c_ref, w_ref, b_ref, o_ref):
    c = c_ref[...]
    s = c * jax.nn.sigmoid(c)
    s_hi = s.astype(BF16)
    s_lo = (s - s_hi.astype(F32)).astype(BF16)
    lhs = jnp.concatenate([s_hi, s_lo], axis=0)
    r = jnp.dot(lhs, w_ref[...].astype(BF16), preferred_element_type=F32)
    rows = c.shape[0]
    o_ref[...] = r[:rows] + r[rows:] + b_ref[...]


def _modulation(c_pad, w3, b2, layer, n_split, batch):
    rows, d = c_pad.shape
    nout = w3.shape[-1]
    tn = min(512, nout)
    out = pl.pallas_call(
        _mod_kernel,
        grid=(nout // tn,),
        in_specs=[
            pl.BlockSpec((rows, d), lambda j: (0, 0)),
            pl.BlockSpec((None, d, tn), lambda j: (layer, 0, j)),
            pl.BlockSpec((None, 1, tn), lambda j: (layer, 0, j)),
        ],
        out_specs=pl.BlockSpec((rows, tn), lambda j: (0, j)),
        out_shape=jax.ShapeDtypeStruct((rows, nout), F32),
        compiler_params=_params("arbitrary"),
        name="modulation",
    )(c_pad, w3, b2.reshape(b2.shape[0], 1, nout))
    return [t[:batch, None, :] for t in jnp.split(out, n_split, axis=-1)]


def _modnorm_kernel(n_out, h_ref, *refs):
    y = _rms(h_ref[...])
    for t in range(n_out):
        g_ref, sh_ref, sc_ref = refs[3 * t:3 * t + 3]
        o_ref = refs[3 * n_out + t]
        o_ref[...] = ((y * g_ref[...]) * (1.0 + sc_ref[...]) + sh_ref[...]).astype(o_ref.dtype)


def _modnorm(h, mods, seq):
    n, d = h.shape
    tm = min(512, seq)
    per_b = seq // tm
    ins, specs = [h], [pl.BlockSpec((tm, d), lambda i: (i, 0))]
    for g, sh, sc in mods:
        ins += [g.reshape(1, d), sh, sc]
        specs += [pl.BlockSpec((1, d), lambda i: (0, 0)),
                  pl.BlockSpec((None, 1, d), lambda i: (i // per_b, 0, 0)),
                  pl.BlockSpec((None, 1, d), lambda i: (i // per_b, 0, 0))]
    outs = pl.pallas_call(
        functools.partial(_modnorm_kernel, len(mods)),
        grid=(n // tm,),
        in_specs=specs,
        out_specs=[pl.BlockSpec((tm, d), lambda i: (i, 0)) for _ in mods],
        out_shape=[jax.ShapeDtypeStruct((n, d), BF16) for _ in mods],
        compiler_params=_params("parallel"),
        name="modnorm",
    )(*ins)
    return list(outs)


def _matmul_kernel(epilogue, scale, x_ref, w_ref, o_ref):
    acc = jnp.dot(x_ref[...], w_ref[...], preferred_element_type=F32)
    if epilogue == "gelu":
        acc = 0.5 * acc * (1.0 + lax.erf(acc * (1.0 / math.sqrt(2.0))))
    elif epilogue == "scale":
        acc = acc * scale
    o_ref[...] = acc.astype(o_ref.dtype)


def _matmul(x, w, out_dtype, epilogue=None, scale=1.0, name="matmul"):
    m, k = x.shape
    nn = w.shape[1]
    tm = min(1024, m)
    tn = min(1024 if out_dtype == BF16 else 512, nn)
    return pl.pallas_call(
        functools.partial(_matmul_kernel, epilogue, scale),
        grid=(m // tm, nn // tn),
        in_specs=[pl.BlockSpec((tm, k), lambda i, j: (i, 0)),
                  pl.BlockSpec((k, tn), lambda i, j: (0, j))],
        out_specs=pl.BlockSpec((tm, tn), lambda i, j: (i, j)),
        out_shape=jax.ShapeDtypeStruct((m, nn), out_dtype),
        compiler_params=_params("parallel", "arbitrary"),
        name=name,
    )(x, w)


def _gate_kernel(chunk, groups, u_ref, v_ref, g_ref, b_ref, ws_ref, bs_ref, o_ref):
    v = v_ref[...].astype(F32)
    mu = jnp.mean(v, axis=-1, keepdims=True)
    vc = v - mu
    var = jnp.mean(vc * vc, axis=-1, keepdims=True)
    vln = (vc * lax.rsqrt(var + LN_EPS) * g_ref[...] + b_ref[...]).astype(BF16)
    tm, width = vln.shape
    gd = width // groups
    causal = (lax.broadcasted_iota(jnp.int32, (chunk, chunk), 0)
              >= lax.broadcasted_iota(jnp.int32, (chunk, chunk), 1))
    for g in range(groups):
        ws = jnp.where(causal, ws_ref[g], 0.0).astype(BF16)
        bias = bs_ref[:, g:g + 1]
        for c in range(tm // chunk):
            rows = slice(c * chunk, (c + 1) * chunk)
            cols = slice(g * gd, (g + 1) * gd)
            sv = jnp.dot(ws, vln[rows, cols], preferred_element_type=F32) + bias
            o_ref[rows, cols] = (u_ref[rows, cols].astype(F32) * sv).astype(o_ref.dtype)


def _gmlp_gate(z, ln_g, ln_b, w_s, b_s):
    n, w2 = z.shape
    width = w2 // 2
    groups, chunk, _ = w_s.shape
    tm = min(2 * chunk, n)
    return pl.pallas_call(
        functools.partial(_gate_kernel, chunk, groups),
        grid=(n // tm,),
        in_specs=[pl.BlockSpec((tm, width), lambda i: (i, 0)),
                  pl.BlockSpec((tm, width), lambda i: (i, 1)),
                  pl.BlockSpec((1, width), lambda i: (0, 0)),
                  pl.BlockSpec((1, width), lambda i: (0, 0)),
                  pl.BlockSpec((groups, chunk, chunk), lambda i: (0, 0, 0)),
                  pl.BlockSpec((chunk, groups), lambda i: (0, 0))],
        out_specs=pl.BlockSpec((tm, width), lambda i: (i, 0)),
        out_shape=jax.ShapeDtypeStruct((n, width), BF16),
        compiler_params=_params("parallel"),
        name="gmlp_gate",
    )(z, z, ln_g.reshape(1, width), ln_b.reshape(1, width), w_s, b_s.T)


def _post_kernel(h_ref, y_ref, g_ref, gate_ref, o_ref):
    o_ref[...] = h_ref[...] + gate_ref[...] * (_rms(y_ref[...]) * g_ref[...])


def _post(h, y, g, gate, seq):
    n, d = h.shape
    tm = min(256, seq)
    per_b = seq // tm
    return pl.pallas_call(
        _post_kernel,
        grid=(n // tm,),
        in_specs=[pl.BlockSpec((tm, d), lambda i: (i, 0)),
                  pl.BlockSpec((tm, d), lambda i: (i, 0)),
                  pl.BlockSpec((1, d), lambda i: (0, 0)),
                  pl.BlockSpec((None, 1, d), lambda i: (i // per_b, 0, 0))],
        out_specs=pl.BlockSpec((tm, d), lambda i: (i, 0)),
        out_shape=jax.ShapeDtypeStruct((n, d), F32),
        compiler_params=_params("parallel"),
        name="post_residual",
    )(h, y, g.reshape(1, d), gate)


ATTN_ROW_CHUNK = 128


def _attn_kernel(tq, tk, dh, lambda_init, q_ref, k_ref, v_ref, lq1_ref, lk1_ref, lq2_ref, lk2_ref,
                 sg_ref, o_ref, *scratch):
    s_ref = (scratch[0:2], scratch[2:4])
    p_ref = (scratch[4:6], scratch[6:8])
    al_ref = (scratch[8:10], scratch[10:12])
    m_ref, l_ref, acc_ref = scratch[12:14], scratch[14:16], scratch[16:18]
    qi = pl.program_id(2)
    dims = (((1,), (1,)), ((), ()))
    rc = min(ATTN_ROW_CHUNK, tq)

    def scores(blk, buf):
        start = pl.multiple_of(blk * tk, tk)
        for mp in range(2):
            s_ref[buf][mp][...] = lax.dot_general(
                q_ref[:, mp * dh:(mp + 1) * dh], k_ref[pl.ds(start, tk), pl.ds(mp * dh, dh)], dims,
                preferred_element_type=F32)

    def accumulate(blk, buf):
        start = pl.multiple_of(blk * tk, tk)
        v = v_ref[pl.ds(start, tk), :]
        for mp in range(2):
            pv = jnp.dot(p_ref[buf][mp][...], v, preferred_element_type=F32)
            alpha = al_ref[buf][mp][...]
            for j in range(2 * dh // LANES):
                cols = slice(j * LANES, (j + 1) * LANES)
                acc_ref[mp][:, cols] = alpha * acc_ref[mp][:, cols] + pv[:, cols]

    def softmax(blk, buf, masked):
        nblk = tk // LANES
        for mp in range(2):
            for c in range(tq // rc):
                rows = slice(c * rc, (c + 1) * rc)
                sb = [s_ref[buf][mp][rows, j * LANES:(j + 1) * LANES] for j in range(nblk)]
                if masked:
                    row = qi * tq + c * rc + lax.broadcasted_iota(jnp.int32, (rc, LANES), 0)
                    col = blk * tk + lax.broadcasted_iota(jnp.int32, (rc, LANES), 1)
                    sb = [jnp.where(col + j * LANES <= row, sb[j], -jnp.inf) for j in range(nblk)]
                part = sb[0]
                for j in range(1, nblk):
                    part = jnp.maximum(part, sb[j])
                m_old = m_ref[mp][rows, :]
                m_new = jnp.maximum(m_old, jnp.max(part, axis=-1, keepdims=True))
                pb = [jnp.exp2(sb[j] - m_new) for j in range(nblk)]
                part = pb[0]
                for j in range(1, nblk):
                    part = part + pb[j]
                alpha = jnp.exp2(m_old - m_new)
                l_ref[mp][rows, :] = alpha * l_ref[mp][rows, :] + jnp.sum(part, axis=-1, keepdims=True)
                m_ref[mp][rows, :] = m_new
                al_ref[buf][mp][rows, :] = alpha
                for j in range(nblk):
                    p_ref[buf][mp][rows, j * LANES:(j + 1) * LANES] = pb[j].astype(BF16)

    for mp in range(2):
        m_ref[mp][...] = jnp.full(m_ref[mp].shape, -jnp.inf, F32)
        l_ref[mp][...] = jnp.zeros(l_ref[mp].shape, F32)
        acc_ref[mp][...] = jnp.zeros(acc_ref[mp].shape, F32)
        p_ref[1][mp][...] = jnp.zeros(p_ref[1][mp].shape, BF16)
        al_ref[1][mp][...] = jnp.ones(al_ref[1][mp].shape, F32)
    scores(0, 0)

    def pair(t, _):
        scores(2 * t + 1, 1)
        accumulate(jnp.maximum(2 * t - 1, 0), 1)
        softmax(2 * t, 0, False)
        scores(2 * t + 2, 0)
        accumulate(2 * t, 0)
        softmax(2 * t + 1, 1, False)
        return 0

    lax.fori_loop(0, qi, pair, 0)
    scores(2 * qi + 1, 1)
    accumulate(jnp.maximum(2 * qi - 1, 0), 1)
    softmax(2 * qi, 0, True)
    accumulate(2 * qi, 0)
    softmax(2 * qi + 1, 1, True)
    accumulate(2 * qi + 1, 1)

    lam = (jnp.exp(jnp.sum(lq1_ref[...] * lk1_ref[...], axis=-1, keepdims=True))
           - jnp.exp(jnp.sum(lq2_ref[...] * lk2_ref[...], axis=-1, keepdims=True)) + lambda_init)
    inv1 = 1.0 / l_ref[0][...]
    inv2 = lam / l_ref[1][...]
    o = jnp.concatenate(
        [acc_ref[0][:, j * LANES:(j + 1) * LANES] * inv1 - acc_ref[1][:, j * LANES:(j + 1) * LANES] * inv2
         for j in range(2 * dh // LANES)], axis=1)
    o = _rms(o) * sg_ref[...] * (1.0 - lambda_init)
    o_ref[...] = o.astype(o_ref.dtype)


def _diff_attention(q, kv, lq1, lk1, lq2, lk2, subln_g, lambda_init):
    b, s, aw = q.shape
    dh = lq1.shape[-1]
    heads = aw // (2 * dh)
    tq = min(1024, s)
    tk = tq // 2
    vec = lambda a: a.reshape(1, -1)
    small = lambda w: pl.BlockSpec((1, w), lambda bi, hi, qi: (0, 0))
    return pl.pallas_call(
        functools.partial(_attn_kernel, tq, tk, dh, lambda_init),
        grid=(b, heads, s // tq),
        in_specs=[pl.BlockSpec((None, tq, 2 * dh), lambda bi, hi, qi: (bi, qi, hi)),
                  pl.BlockSpec((None, s, 2 * dh), lambda bi, hi, qi: (bi, 0, hi)),
                  pl.BlockSpec((None, s, 2 * dh), lambda bi, hi, qi: (bi, 0, heads + hi)),
                  small(dh), small(dh), small(dh), small(dh), small(2 * dh)],
        out_specs=pl.BlockSpec((None, tq, 2 * dh), lambda bi, hi, qi: (bi, qi, hi)),
        out_shape=jax.ShapeDtypeStruct((b, s, aw), BF16),
        scratch_shapes=([pltpu.VMEM((tq, tk), F32)] * 4 + [pltpu.VMEM((tq, tk), BF16)] * 4
                        + [pltpu.VMEM((tq, LANES), F32)] * 8 + [pltpu.VMEM((tq, 2 * dh), F32)] * 2),
        compiler_params=_params("parallel", "parallel", "arbitrary"),
        name="diff_attention",
    )(q, kv, kv, vec(lq1), vec(lk1), vec(lq2), vec(lk2), vec(subln_g))


def _router_kernel(h_ref, g_ref, sh_ref, sc_ref, wr_ref, br_ref,
                   np_ref, idx_ref, gate_ref, rank_ref, cnt_ref, carry_ref):
    i = pl.program_id(0)

    @pl.when(i == 0)
    def _():
        carry_ref[...] = jnp.zeros_like(carry_ref)

    n = (_rms(h_ref[...]) * g_ref[...]) * (1.0 + sc_ref[...]) + sh_ref[...]
    tm, d = n.shape
    n_hi = n.astype(BF16)
    n_hi32 = n_hi.astype(F32)
    n_lo = (n - n_hi32).astype(BF16)

    bits = pltpu.bitcast(n_hi32, jnp.uint32)
    half = d // 2
    np_ref[...] = (lax.shift_right_logical(bits[:, :half], jnp.uint32(16))
                   | (bits[:, half:] & jnp.uint32(0xFFFF0000)))

    wr = wr_ref[...]
    w_hi = wr.astype(BF16)
    w_lo = (wr - w_hi.astype(F32)).astype(BF16)
    dims = (((1,), (1,)), ((), ()))
    logits = (lax.dot_general(w_hi, n_hi, dims, preferred_element_type=F32)
              + lax.dot_general(w_lo, n_hi, dims, preferred_element_type=F32)
              + lax.dot_general(w_hi, n_lo, dims, preferred_element_type=F32)
              + br_ref[...])
    n_exp = logits.shape[0]
    eid = lax.broadcasted_iota(jnp.int32, (n_exp, tm), 0).astype(F32)

    vals, sels, ids = [], [], []
    rem = logits
    for _ in range(TOP_K):
        m = jnp.max(rem, axis=0, keepdims=True)
        first = jnp.min(jnp.where(rem == m, eid, float(n_exp)), axis=0, keepdims=True)
        sel = eid == first
        rem = jnp.where(sel, -jnp.inf, rem)
        vals.append(m)
        sels.append(sel)
        ids.append(first)

    exps = [jnp.exp(v - vals[0]) for v in vals]
    denom = exps[0]
    for e in exps[1:]:
        denom = denom + e

    chosen = sels[0]
    for sel in sels[1:]:
        chosen = jnp.logical_or(chosen, sel)
    chosen = jnp.where(chosen, 1.0, 0.0)
    before = (lax.broadcasted_iota(jnp.int32, (tm, tm), 0)
              < lax.broadcasted_iota(jnp.int32, (tm, tm), 1))
    upper = jnp.where(before, 1.0, 0.0).astype(BF16)
    rank_excl = jnp.dot(chosen.astype(BF16), upper, preferred_element_type=F32) + carry_ref[...]

    for k in range(TOP_K):
        idx_ref[k:k + 1, :] = ids[k].astype(jnp.int32)
        gate_ref[k:k + 1, :] = exps[k] / denom
        rank_ref[k:k + 1, :] = jnp.sum(jnp.where(sels[k], rank_excl, 0.0), axis=0,
                                       keepdims=True).astype(jnp.int32)

    total = carry_ref[...] + jnp.sum(chosen, axis=1, keepdims=True)
    carry_ref[...] = total
    cnt_ref[...] = jnp.broadcast_to(total, cnt_ref.shape).astype(jnp.int32)


def _router(h, g, shift, scale, w_r, b_r, seq):
    n, d = h.shape
    n_exp = w_r.shape[1]
    tm = min(512, seq)
    per_b = seq // tm
    tok = lambda dt: jax.ShapeDtypeStruct((TOP_K, n), dt)
    tok_spec = pl.BlockSpec((TOP_K, tm), lambda i: (0, i))
    return pl.pallas_call(
        _router_kernel,
        grid=(n // tm,),
        in_specs=[pl.BlockSpec((tm, d), lambda i: (i, 0)),
                  pl.BlockSpec((1, d), lambda i: (0, 0)),
                  pl.BlockSpec((None, 1, d), lambda i: (i // per_b, 0, 0)),
                  pl.BlockSpec((None, 1, d), lambda i: (i // per_b, 0, 0)),
                  pl.BlockSpec((n_exp, d), lambda i: (0, 0)),
                  pl.BlockSpec((n_exp, 1), lambda i: (0, 0))],
        out_specs=[pl.BlockSpec((tm, d // 2), lambda i: (i, 0)), tok_spec, tok_spec, tok_spec,
                   pl.BlockSpec((n_exp, LANES), lambda i: (0, 0))],
        out_shape=[jax.ShapeDtypeStruct((n, d // 2), jnp.uint32), tok(jnp.int32), tok(F32),
                   tok(jnp.int32), jax.ShapeDtypeStruct((n_exp, LANES), jnp.int32)],
        scratch_shapes=[pltpu.VMEM((n_exp, 1), F32)],
        compiler_params=_params("arbitrary"),
        name="moe_router",
    )(h, g.reshape(1, d), shift, scale, w_r.T, b_r.reshape(n_exp, 1))


def _dispatch_kernel(n_tok, tm, dest_ref, ztile_ref, np_ref, xs_ref, zeros, sem, zsem):
    i = pl.program_id(0)

    @pl.when(i == 0)
    def _():
        zeros[...] = jnp.zeros_like(zeros)
        rows = zeros.shape[0]

        def fill(t):
            return pltpu.make_async_copy(zeros, xs_ref.at[pl.ds(t * rows, rows)], zsem)

        def start(t, _):
            @pl.when(ztile_ref[t] == 1)
            def _():
                fill(t).start()
            return 0

        def finish(t, _):
            @pl.when(ztile_ref[t] == 1)
            def _():
                fill(t).wait()
            return 0

        n_tiles = xs_ref.shape[0] // rows
        lax.fori_loop(0, n_tiles, start, 0)
        lax.fori_loop(0, n_tiles, finish, 0)

    def row_copy(r, k):
        d = dest_ref[k * n_tok + i * tm + r]
        return pltpu.make_async_copy(np_ref.at[pl.ds(r, 1)], xs_ref.at[pl.ds(d, 1)], sem)

    def issue(r, _):
        for k in range(TOP_K):
            row_copy(r, k).start()
        return 0

    lax.fori_loop(0, tm, issue, 0)

    def drain(r, _):
        for k in range(TOP_K):
            row_copy(r, k).wait()
        return 0

    lax.fori_loop(0, tm, drain, 0)


def _dispatch(n_packed, dest_flat, zero_tile):
    n, half = n_packed.shape
    tm = min(256, n)
    n_rows = zero_tile.shape[0] * MOE_TILE
    return pl.pallas_call(
        functools.partial(_dispatch_kernel, n, tm),
        grid_spec=pltpu.PrefetchScalarGridSpec(
            num_scalar_prefetch=2,
            grid=(n // tm,),
            in_specs=[pl.BlockSpec((tm, half), lambda i, dest, zt: (i, 0))],
            out_specs=pl.BlockSpec(memory_space=pl.ANY),
            scratch_shapes=[pltpu.VMEM((MOE_TILE, half), jnp.uint32),
                            pltpu.SemaphoreType.DMA(()), pltpu.SemaphoreType.DMA(())]),
        out_shape=jax.ShapeDtypeStruct((n_rows, half), jnp.uint32),
        compiler_params=_params("arbitrary"),
        name="moe_dispatch",
    )(dest_flat, zero_tile, n_packed)


def _w1_prep_kernel(group, w_ref, p_ref, o_ref):
    for cb in range(w_ref.shape[1] // group):
        cols = slice(cb * group, (cb + 1) * group)
        o_ref[:, cols] = jnp.dot(w_ref[:, cols].astype(BF16), p_ref[...],
                                 preferred_element_type=F32).astype(BF16)


def _w1_prep(w1):
    shape = w1.shape
    cols = shape[-1]
    rows = math.prod(shape[:-1])
    group = min(2 * LANES, cols)
    tm = min(2048, rows)
    src = lax.broadcasted_iota(jnp.int32, (group, group), 0)
    dst = lax.broadcasted_iota(jnp.int32, (group, group), 1)
    perm = (dst == src // 2 + (group // 2) * (src % 2)).astype(BF16)
    out = pl.pallas_call(
        functools.partial(_w1_prep_kernel, group),
        grid=(rows // tm,),
        in_specs=[pl.BlockSpec((tm, cols), lambda i: (i, 0)),
                  pl.BlockSpec((group, group), lambda i: (0, 0))],
        out_specs=pl.BlockSpec((tm, cols), lambda i: (i, 0)),
        out_shape=jax.ShapeDtypeStruct((rows, cols), BF16),
        compiler_params=_params("parallel"),
        name="moe_w1_prep",
    )(w1.reshape(rows, cols), perm)
    return out.reshape(shape)


def _deinterleave_blocks(h, group):
    half = group // 2
    blocks = range(h.shape[1] // group)
    glu = jnp.concatenate([h[:, b * group:b * group + half] for b in blocks], axis=1)
    lin = jnp.concatenate([h[:, b * group + half:(b + 1) * group] for b in blocks], axis=1)
    return glu, lin


def _expert_kernel(group, texp_ref, nused_ref, x_ref, w1_ref, w2_ref, b1g_ref, b1l_ref, b2_ref, o_ref):
    i = pl.program_id(0)

    @pl.when(i < nused_ref[0])
    def _():
        w = x_ref[...]
        half = w.shape[1]
        x_lo = pltpu.bitcast(lax.shift_left(w, jnp.uint32(16)), F32).astype(BF16)
        x_hi = pltpu.bitcast(w & jnp.uint32(0xFFFF0000), F32).astype(BF16)
        hcat = (jnp.dot(x_lo, w1_ref[:half, :], preferred_element_type=F32)
                + jnp.dot(x_hi, w1_ref[half:, :], preferred_element_type=F32))
        glu, lin = _deinterleave_blocks(hcat, group)
        glu = jnp.minimum(glu + b1g_ref[...], SWIGLU_LIMIT)
        lin = jnp.clip(lin + b1l_ref[...], -SWIGLU_LIMIT, SWIGLU_LIMIT)
        act = glu * jax.nn.sigmoid(SWIGLU_ALPHA * glu) * (lin + 1.0)
        o_ref[...] = jnp.dot(act.astype(BF16), w2_ref[...], preferred_element_type=F32) + b2_ref[...]

    @pl.when(i >= nused_ref[0])
    def _():
        o_ref[...] = jnp.zeros_like(o_ref)


def _experts(xs, tile_exp, n_used, w1p, layer, w2, b1g, b1l, b2):
    n_rows, half = xs.shape
    _, n_exp, d, ff2 = w1p.shape
    ff = ff2 // 2
    n_tiles = n_rows // MOE_TILE
    tile = lambda i, te, nu: (jnp.minimum(i, nu[0] - 1), 0)
    wmap = lambda i, te, nu: (te[jnp.minimum(i, nu[0] - 1)], 0, 0)
    w1map = lambda i, te, nu: (layer, te[jnp.minimum(i, nu[0] - 1)], 0, 0)
    return pl.pallas_call(
        functools.partial(_expert_kernel, min(2 * LANES, ff2)),
        grid_spec=pltpu.PrefetchScalarGridSpec(
            num_scalar_prefetch=2,
            grid=(n_tiles,),
            in_specs=[pl.BlockSpec((MOE_TILE, half), tile),
                      pl.BlockSpec((None, None, d, ff2), w1map),
                      pl.BlockSpec((None, ff, d), wmap),
                      pl.BlockSpec((None, 1, ff), wmap),
                      pl.BlockSpec((None, 1, ff), wmap),
                      pl.BlockSpec((None, 1, d), wmap)],
            out_specs=pl.BlockSpec((MOE_TILE, d), lambda i, te, nu: (i, 0))),
        out_shape=jax.ShapeDtypeStruct((n_rows, d), F32),
        compiler_params=_params("arbitrary"),
        name="moe_experts",
    )(tile_exp, n_used, xs, w1p, w2, b1g, b1l, b2)


def _combine_kernel(n_tok, tm, dest_ref, yb_ref, gt_ref, h_ref, g_ref, gate_ref, o_ref, buf, sems):
    i = pl.program_id(0)
    steps = pl.num_programs(0)

    def row_copy(step, slot, r, k):
        d = dest_ref[k * n_tok + step * tm + r]
        return pltpu.make_async_copy(yb_ref.at[pl.ds(d, 1)], buf.at[slot, k, pl.ds(r, 1)], sems.at[slot])

    def issue(step, slot):
        def body(r, _):
            for k in range(TOP_K):
                row_copy(step, slot, r, k).start()
            return 0
        lax.fori_loop(0, tm, body, 0)

    slot = i % 2

    @pl.when(i == 0)
    def _():
        issue(0, 0)

    @pl.when(i + 1 < steps)
    def _():
        issue(i + 1, 1 - slot)

    def drain(r, _):
        for k in range(TOP_K):
            row_copy(i, slot, r, k).wait()
        return 0

    lax.fori_loop(0, tm, drain, 0)

    y = buf[slot, 0] * gt_ref[:, 0:1]
    for k in range(1, TOP_K):
        y = y + buf[slot, k] * gt_ref[:, k:k + 1]
    o_ref[...] = h_ref[...] + gate_ref[...] * (_rms(y) * g_ref[...])


def _combine(yb, dest_flat, gates_t, h, g, gate, seq):
    n, d = h.shape
    tm = min(128, seq)
    per_b = seq // tm
    return pl.pallas_call(
        functools.partial(_combine_kernel, n, tm),
        grid_spec=pltpu.PrefetchScalarGridSpec(
            num_scalar_prefetch=1,
            grid=(n // tm,),
            in_specs=[pl.BlockSpec(memory_space=pl.ANY),
                      pl.BlockSpec((tm, TOP_K), lambda i, dest: (i, 0)),
                      pl.BlockSpec((tm, d), lambda i, dest: (i, 0)),
                      pl.BlockSpec((1, d), lambda i, dest: (0, 0)),
                      pl.BlockSpec((None, 1, d), lambda i, dest: (i // per_b, 0, 0))],
            out_specs=pl.BlockSpec((tm, d), lambda i, dest: (i, 0)),
            scratch_shapes=[pltpu.VMEM((2, TOP_K, tm, d), F32), pltpu.SemaphoreType.DMA((2,))]),
        out_shape=jax.ShapeDtypeStruct((n, d), F32),
        compiler_params=_params("arbitrary"),
        name="moe_combine",
    )(dest_flat, yb, gates_t, h, g.reshape(1, d), gate)


def _moe_layer(h, seq, pre_g, post_g, shift, scale, gate, w_r, b_r, w1p, layer, b1, w2, b2):
    n, d = h.shape
    n_exp = w_r.shape[1]
    n_packed, idx, gates, rank, cnt = _router(h, pre_g, shift, scale, w_r, b_r, seq)

    counts = cnt[:, 0]
    tiles_e = (counts + MOE_TILE - 1) // MOE_TILE
    tile_end = jnp.cumsum(tiles_e)
    row_start = (tile_end - tiles_e) * MOE_TILE
    experts = jnp.arange(n_exp, dtype=jnp.int32)[:, None, None]
    dest = jnp.sum(jnp.where(idx[None] == experts, row_start[:, None, None], 0), axis=0) + rank
    dest = dest.reshape(-1).astype(jnp.int32)
    n_tiles = (n * TOP_K) // MOE_TILE + n_exp
    tile_exp = jnp.sum(tile_end[None, :] <= jnp.arange(n_tiles)[:, None], axis=1)
    tile_exp = jnp.minimum(tile_exp, n_exp - 1).astype(jnp.int32)
    n_used = tile_end[-1:].astype(jnp.int32)
    tile_ids = jnp.arange(n_tiles)
    last_of_expert = jnp.any((tile_ids[:, None] == tile_end[None, :] - 1) & (tiles_e[None, :] > 0), axis=1)
    zero_tile = (last_of_expert | (tile_ids >= n_used[0])).astype(jnp.int32)

    xs = _dispatch(n_packed, dest, zero_tile)
    ff = w2.shape[1]
    yb = _experts(xs, tile_exp, n_used, w1p, layer, w2.astype(BF16),
                  b1[:, 0::2].reshape(n_exp, 1, ff), b1[:, 1::2].reshape(n_exp, 1, ff),
                  b2.reshape(n_exp, 1, d))
    return _combine(yb, dest, gates.T, h, post_g, gate, seq)


def _lambda_init(layer):
    return 0.8 - 0.6 * math.exp(-0.3 * layer)


def kernel(x, c, mix_pre_g, mix_post_g, mix_mod_w, mix_mod_b, ffn_pre_g, ffn_post_g, ffn_mod_w, ffn_mod_b, a_w_in, a_ln_g, a_ln_b, a_w_s, a_b_s, a_w_out, kv_g, kv_mod_w, kv_mod_b, w_k, w_v, b_w_q, b_lq1, b_lk1, b_lq2, b_lk2, b_subln_g, b_w_o, moe_w_r, moe_b_r, moe_w1, moe_b1, moe_w2, moe_b2):
    batch, seq, d = x.shape
    depth = mix_pre_g.shape[0]
    n_a = a_w_in.shape[0]
    dh = b_lq1.shape[-1]
    h = x.reshape(batch * seq, d)
    c_pad = jnp.zeros((16, d), F32).at[:batch].set(c)
    moe_w1p = _w1_prep(moe_w1)

    kv = None
    for l in range(depth):
        shift, scale, gate = _modulation(c_pad, mix_mod_w, mix_mod_b, l, 3, batch)
        if l < n_a:
            (n,) = _modnorm(h, [(mix_pre_g[l], shift, scale)], seq)
            z = _matmul(n, a_w_in[l].astype(BF16), BF16, epilogue="gelu", name="gmlp_in")
            gated = _gmlp_gate(z, a_ln_g[l], a_ln_b[l], a_w_s[l], a_b_s[l])
            out = _matmul(gated, a_w_out[l].astype(BF16), F32, name="gmlp_out")
        else:
            j = l - n_a
            if kv is None:
                kv_shift, kv_scale = _modulation(c_pad, kv_mod_w[None], kv_mod_b[None], 0, 2, batch)
                n_kv, n = _modnorm(h, [(kv_g, kv_shift, kv_scale), (mix_pre_g[l], shift, scale)], seq)
                w_kv = jnp.concatenate([w_k, w_v], axis=1).astype(BF16)
                kv = _matmul(n_kv, w_kv, BF16, name="kv_proj").reshape(batch, seq, -1)
            else:
                (n,) = _modnorm(h, [(mix_pre_g[l], shift, scale)], seq)
            q = _matmul(n, b_w_q[j].astype(BF16), BF16, epilogue="scale",
                        scale=LOG2E * dh ** -0.5, name="q_proj").reshape(batch, seq, -1)
            o = _diff_attention(q, kv, b_lq1[j], b_lk1[j], b_lq2[j], b_lk2[j], b_subln_g[j],
                                _lambda_init(l))
            out = _matmul(o.reshape(batch * seq, -1), b_w_o[j].astype(BF16), F32, name="attn_out")
        h = _post(h, out, mix_post_g[l], gate, seq)

        shift, scale, gate = _modulation(c_pad, ffn_mod_w, ffn_mod_b, l, 3, batch)
        h = _moe_layer(h, seq, ffn_pre_g[l], ffn_post_g[l], shift, scale, gate,
                       moe_w_r[l], moe_b_r[l], moe_w1p, l, moe_b1[l], moe_w2[l], moe_b2[l])
    return h.reshape(batch, seq, d)
```

```python
import functools
import math

import jax
import jax.numpy as jnp
from jax import lax
from jax.experimental import pallas as pl
from jax.experimental.pallas import tpu as pltpu

RMS_EPS = 1e-6
LN_EPS = 1e-5
TOP_K = 4
SWIGLU_ALPHA = 1.702
SWIGLU_LIMIT = 7.0
LOG2E = 1.4426950408889634

VMEM_LIMIT_BYTES = 56 * 1024 * 1024
LANES = 128
MOE_TILE = 256

BF16 = jnp.bfloat16
F32 = jnp.float32


def _params(*sem):
    return pltpu.CompilerParams(dimension_semantics=sem, vmem_limit_bytes=VMEM_LIMIT_BYTES)


def _rms(x):
    return x * lax.rsqrt(jnp.mean(x * x, axis=-1, keepdims=True) + RMS_EPS)


def _mod_kernel(c_ref, w_ref, b_ref, o_ref):
    c = c_ref[...]
    s = c * jax.nn.sigmoid(c)
    s_hi = s.astype(BF16)
    s_lo = (s - s_hi.astype(F32)).astype(BF16)
    lhs = jnp.concatenate([s_hi, s_lo], axis=0)
    r = jnp.dot(lhs, w_ref[...].astype(BF16), preferred_element_type=F32)
    rows = c.shape[0]
    o_ref[...] = r[:rows] + r[rows:] + b_ref[...]


def _modulation(c_pad, w3, b2, layer, n_split, batch):
    rows, d = c_pad.shape
    nout = w3.shape[-1]
    tn = min(512, nout)
    out = pl.pallas_call(
        _mod_kernel,
        grid=(nout // tn,),
        in_specs=[
            pl.BlockSpec((rows, d), lambda j: (0, 0)),
            pl.BlockSpec((None, d, tn), lambda j: (layer, 0, j)),
            pl.BlockSpec((None, 1, tn), lambda j: (layer, 0, j)),
        ],
        out_specs=pl.BlockSpec((rows, tn), lambda j: (0, j)),
        out_shape=jax.ShapeDtypeStruct((rows, nout), F32),
        compiler_params=_params("arbitrary"),
        name="modulation",
    )(c_pad, w3, b2.reshape(b2.shape[0], 1, nout))
    return [t[:batch, None, :] for t in jnp.split(out, n_split, axis=-1)]


def _modnorm_kernel(n_out, h_ref, *refs):
    y = _rms(h_ref[...])
    for t in range(n_out):
        g_ref, sh_ref, sc_ref = refs[3 * t:3 * t + 3]
        o_ref = refs[3 * n_out + t]
        o_ref[...] = ((y * g_ref[...]) * (1.0 + sc_ref[...]) + sh_ref[...]).astype(o_ref.dtype)


def _modnorm(h, mods, seq):
    n, d = h.shape
    tm = min(512, seq)
    per_b = seq // tm
    ins, specs = [h], [pl.BlockSpec((tm, d), lambda i: (i, 0))]
    for g, sh, sc in mods:
        ins += [g.reshape(1, d), sh, sc]
        specs += [pl.BlockSpec((1, d), lambda i: (0, 0)),
                  pl.BlockSpec((None, 1, d), lambda i: (i // per_b, 0, 0)),
                  pl.BlockSpec((None, 1, d), lambda i: (i // per_b, 0, 0))]
    outs = pl.pallas_call(
        functools.partial(_modnorm_kernel, len(mods)),
        grid=(n // tm,),
        in_specs=specs,
        out_specs=[pl.BlockSpec((tm, d), lambda i: (i, 0)) for _ in mods],
        out_shape=[jax.ShapeDtypeStruct((n, d), BF16) for _ in mods],
        compiler_params=_params("parallel"),
        name="modnorm",
    )(*ins)
    return list(outs)


def _matmul_kernel(epilogue, scale, x_ref, w_ref, o_ref):
    acc = jnp.dot(x_ref[...], w_ref[...], preferred_element_type=F32)
    if epilogue == "gelu":
        acc = 0.5 * acc * (1.0 + lax.erf(acc * (1.0 / math.sqrt(2.0))))
    elif epilogue == "scale":
        acc = acc * scale
    o_ref[...] = acc.astype(o_ref.dtype)


def _matmul(x, w, out_dtype, epilogue=None, scale=1.0, name="matmul"):
    m, k = x.shape
    nn = w.shape[1]
    tm = min(1024, m)
    tn = min(1024 if out_dtype == BF16 else 512, nn)
    return pl.pallas_call(
        functools.partial(_matmul_kernel, epilogue, scale),
        grid=(m // tm, nn // tn),
        in_specs=[pl.BlockSpec((tm, k), lambda i, j: (i, 0)),
                  pl.BlockSpec((k, tn), lambda i, j: (0, j))],
        out_specs=pl.BlockSpec((tm, tn), lambda i, j: (i, j)),
        out_shape=jax.ShapeDtypeStruct((m, nn), out_dtype),
        compiler_params=_params("parallel", "arbitrary"),
        name=name,
    )(x, w)


def _gate_kernel(chunk, groups, u_ref, v_ref, g_ref, b_ref, ws_ref, bs_ref, o_ref):
    v = v_ref[...].astype(F32)
    mu = jnp.mean(v, axis=-1, keepdims=True)
    vc = v - mu
    var = jnp.mean(vc * vc, axis=-1, keepdims=True)
    vln = (vc * lax.rsqrt(var + LN_EPS) * g_ref[...] + b_ref[...]).astype(BF16)
    tm, width = vln.shape
    gd = width // groups
    causal = (lax.broadcasted_iota(jnp.int32, (chunk, chunk), 0)
              >= lax.broadcasted_iota(jnp.int32, (chunk, chunk), 1))
    for g in range(groups):
        ws = jnp.where(causal, ws_ref[g], 0.0).astype(BF16)
        bias = bs_ref[:, g:g + 1]
        for c in range(tm // chunk):
            rows = slice(c * chunk, (c + 1) * chunk)
            cols = slice(g * gd, (g + 1) * gd)
            sv = jnp.dot(ws, vln[rows, cols], preferred_element_type=F32) + bias
            o_ref[rows, cols] = (u_ref[rows, cols].astype(F32) * sv).astype(o_ref.dtype)


def _gmlp_gate(z, ln_g, ln_b, w_s, b_s):
    n, w2 = z.shape
    width = w2 // 2
    groups, chunk, _ = w_s.shape
    tm = min(2 * chunk, n)
    return pl.pallas_call(
        functools.partial(_gate_kernel, chunk, groups),
        grid=(n // tm,),
        in_specs=[pl.BlockSpec((tm, width), lambda i: (i, 0)),
                  pl.BlockSpec((tm, width), lambda i: (i, 1)),
                  pl.BlockSpec((1, width), lambda i: (0, 0)),
                  pl.BlockSpec((1, width), lambda i: (0, 0)),
                  pl.BlockSpec((groups, chunk, chunk), lambda i: (0, 0, 0)),
                  pl.BlockSpec((chunk, groups), lambda i: (0, 0))],
        out_specs=pl.BlockSpec((tm, width), lambda i: (i, 0)),
        out_shape=jax.ShapeDtypeStruct((n, width), BF16),
        compiler_params=_params("parallel"),
        name="gmlp_gate",
    )(z, z, ln_g.reshape(1, width), ln_b.reshape(1, width), w_s, b_s.T)


def _post_kernel(h_ref, y_ref, g_ref, gate_ref, o_ref):
    o_ref[...] = h_ref[...] + gate_ref[...] * (_rms(y_ref[...]) * g_ref[...])


def _post(h, y, g, gate, seq):
    n, d = h.shape
    tm = min(256, seq)
    per_b = seq // tm
    return pl.pallas_call(
        _post_kernel,
        grid=(n // tm,),
        in_specs=[pl.BlockSpec((tm, d), lambda i: (i, 0)),
                  pl.BlockSpec((tm, d), lambda i: (i, 0)),
                  pl.BlockSpec((1, d), lambda i: (0, 0)),
                  pl.BlockSpec((None, 1, d), lambda i: (i // per_b, 0, 0))],
        out_specs=pl.BlockSpec((tm, d), lambda i: (i, 0)),
        out_shape=jax.ShapeDtypeStruct((n, d), F32),
        compiler_params=_params("parallel"),
        name="post_residual",
    )(h, y, g.reshape(1, d), gate)


ATTN_ROW_CHUNK = 256


def _attn_kernel(tq, tk, dh, lambda_init, q_ref, k_ref, v_ref, lq1_ref, lk1_ref, lq2_ref, lk2_ref,
                 sg_ref, o_ref, *scratch):
    s_ref = (scratch[0:2], scratch[2:4])
    p_ref = (scratch[4:6], scratch[6:8])
    al_ref = (scratch[8:10], scratch[10:12])
    m_ref, l_ref, acc_ref = scratch[12:14], scratch[14:16], scratch[16:18]
    qi = pl.program_id(2)
    dims = (((1,), (1,)), ((), ()))
    rc = min(ATTN_ROW_CHUNK, tq)

    def live_rows(diag):
        return slice(tk, tq) if diag == 1 else slice(0, tq)

    def scores(blk, buf, diag=None):
        start = pl.multiple_of(blk * tk, tk)
        rows = live_rows(diag)
        for mp in range(2):
            s_ref[buf][mp][rows, :] = lax.dot_general(
                q_ref[rows, mp * dh:(mp + 1) * dh], k_ref[pl.ds(start, tk), pl.ds(mp * dh, dh)], dims,
                preferred_element_type=F32)

    def accumulate(blk, buf, diag=None):
        start = pl.multiple_of(blk * tk, tk)
        rows = live_rows(diag)
        v = v_ref[pl.ds(start, tk), :]
        for mp in range(2):
            alpha = jnp.tile(al_ref[buf][mp][rows, :], (1, 2 * dh // LANES))
            acc_ref[mp][rows, :] = alpha * acc_ref[mp][rows, :] + jnp.dot(
                p_ref[buf][mp][rows, :], v, preferred_element_type=F32)

    def softmax(buf, diag=None):
        nblk = tk // LANES
        first = 0 if diag != 1 else tk // rc
        for mp in range(2):
            for c in range(first, tq // rc):
                rows = slice(c * rc, (c + 1) * rc)
                sb = []
                for j in range(nblk):
                    col_lo = (diag or 0) * tk + j * LANES
                    if diag is not None and col_lo > c * rc + rc - 1:
                        sb.append(None)
                        continue
                    s = s_ref[buf][mp][rows, j * LANES:(j + 1) * LANES]
                    if diag is not None and col_lo + LANES - 1 > c * rc:
                        keep = (lax.broadcasted_iota(jnp.int32, (rc, LANES), 1) + (col_lo - c * rc)
                                <= lax.broadcasted_iota(jnp.int32, (rc, LANES), 0))
                        s = jnp.where(keep, s, -jnp.inf)
                    sb.append(s)
                live = [s for s in sb if s is not None]
                part = live[0]
                for s in live[1:]:
                    part = jnp.maximum(part, s)
                m_old = m_ref[mp][rows, :]
                m_new = jnp.maximum(m_old, jnp.max(part, axis=-1, keepdims=True))
                pb = [None if s is None else jnp.exp2(s - m_new) for s in sb]
                live = [p for p in pb if p is not None]
                part = live[0]
                for p in live[1:]:
                    part = part + p
                alpha = jnp.exp2(m_old - m_new)
                l_ref[mp][rows, :] = alpha * l_ref[mp][rows, :] + jnp.sum(part, axis=-1, keepdims=True)
                m_ref[mp][rows, :] = m_new
                al_ref[buf][mp][rows, :] = alpha
                for j in range(nblk):
                    p_ref[buf][mp][rows, j * LANES:(j + 1) * LANES] = (
                        jnp.zeros((rc, LANES), BF16) if pb[j] is None else pb[j].astype(BF16))

    for mp in range(2):
        m_ref[mp][...] = jnp.full(m_ref[mp].shape, -jnp.inf, F32)
        l_ref[mp][...] = jnp.zeros(l_ref[mp].shape, F32)
        acc_ref[mp][...] = jnp.zeros(acc_ref[mp].shape, F32)
        p_ref[1][mp][...] = jnp.zeros(p_ref[1][mp].shape, BF16)
        al_ref[1][mp][...] = jnp.ones(al_ref[1][mp].shape, F32)
    scores(0, 0)

    def pair(t, _):
        scores(2 * t + 1, 1)
        accumulate(jnp.maximum(2 * t - 1, 0), 1)
        softmax(0)
        scores(2 * t + 2, 0)
        accumulate(2 * t, 0)
        softmax(1)
        return 0

    lax.fori_loop(0, qi, pair, 0)
    scores(2 * qi + 1, 1, diag=1)
    accumulate(jnp.maximum(2 * qi - 1, 0), 1)
    softmax(0, diag=0)
    accumulate(2 * qi, 0)
    softmax(1, diag=1)
    accumulate(2 * qi + 1, 1, diag=1)

    lam = (jnp.exp(jnp.sum(lq1_ref[...] * lk1_ref[...], axis=-1, keepdims=True))
           - jnp.exp(jnp.sum(lq2_ref[...] * lk2_ref[...], axis=-1, keepdims=True)) + lambda_init)
    inv1 = 1.0 / l_ref[0][...]
    inv2 = lam / l_ref[1][...]
    o = jnp.concatenate(
        [acc_ref[0][:, j * LANES:(j + 1) * LANES] * inv1 - acc_ref[1][:, j * LANES:(j + 1) * LANES] * inv2
         for j in range(2 * dh // LANES)], axis=1)
    o = _rms(o) * sg_ref[...] * (1.0 - lambda_init)
    o_ref[...] = o.astype(o_ref.dtype)


def _diff_attention(q, kv, lq1, lk1, lq2, lk2, subln_g, lambda_init):
    b, s, aw = q.shape
    dh = lq1.shape[-1]
    heads = aw // (2 * dh)
    tq = min(1024, s)
    tk = tq // 2
    vec = lambda a: a.reshape(1, -1)
    small = lambda w: pl.BlockSpec((1, w), lambda bi, hi, qi: (0, 0))
    return pl.pallas_call(
        functools.partial(_attn_kernel, tq, tk, dh, lambda_init),
        grid=(b, heads, s // tq),
        in_specs=[pl.BlockSpec((None, tq, 2 * dh), lambda bi, hi, qi: (bi, qi, hi)),
                  pl.BlockSpec((None, s, 2 * dh), lambda bi, hi, qi: (bi, 0, hi)),
                  pl.BlockSpec((None, s, 2 * dh), lambda bi, hi, qi: (bi, 0, heads + hi)),
                  small(dh), small(dh), small(dh), small(dh), small(2 * dh)],
        out_specs=pl.BlockSpec((None, tq, 2 * dh), lambda bi, hi, qi: (bi, qi, hi)),
        out_shape=jax.ShapeDtypeStruct((b, s, aw), BF16),
        scratch_shapes=([pltpu.VMEM((tq, tk), F32)] * 4 + [pltpu.VMEM((tq, tk), BF16)] * 4
                        + [pltpu.VMEM((tq, LANES), F32)] * 8 + [pltpu.VMEM((tq, 2 * dh), F32)] * 2),
        compiler_params=_params("parallel", "parallel", "arbitrary"),
        name="diff_attention",
    )(q, kv, kv, vec(lq1), vec(lk1), vec(lq2), vec(lk2), vec(subln_g))


def _router_kernel(h_ref, g_ref, sh_ref, sc_ref, wr_ref, br_ref,
                   np_ref, idx_ref, gate_ref, rank_ref, cnt_ref, carry_ref):
    i = pl.program_id(0)

    @pl.when(i == 0)
    def _():
        carry_ref[...] = jnp.zeros_like(carry_ref)

    n = (_rms(h_ref[...]) * g_ref[...]) * (1.0 + sc_ref[...]) + sh_ref[...]
    tm, d = n.shape
    n_hi = n.astype(BF16)
    n_hi32 = n_hi.astype(F32)
    n_lo = (n - n_hi32).astype(BF16)

    bits = pltpu.bitcast(n_hi32, jnp.uint32)
    half = d // 2
    np_ref[...] = (lax.shift_right_logical(bits[:, :half], jnp.uint32(16))
                   | (bits[:, half:] & jnp.uint32(0xFFFF0000)))

    wr = wr_ref[...]
    w_hi = wr.astype(BF16)
    w_lo = (wr - w_hi.astype(F32)).astype(BF16)
    dims = (((1,), (1,)), ((), ()))
    logits = (lax.dot_general(w_hi, n_hi, dims, preferred_element_type=F32)
              + lax.dot_general(w_lo, n_hi, dims, preferred_element_type=F32)
              + lax.dot_general(w_hi, n_lo, dims, preferred_element_type=F32)
              + br_ref[...])
    n_exp = logits.shape[0]
    eid = lax.broadcasted_iota(jnp.int32, (n_exp, tm), 0).astype(F32)

    vals, sels, ids = [], [], []
    rem = logits
    for _ in range(TOP_K):
        m = jnp.max(rem, axis=0, keepdims=True)
        first = jnp.min(jnp.where(rem == m, eid, float(n_exp)), axis=0, keepdims=True)
        sel = eid == first
        rem = jnp.where(sel, -jnp.inf, rem)
        vals.append(m)
        sels.append(sel)
        ids.append(first)

    exps = [jnp.exp(v - vals[0]) for v in vals]
    denom = exps[0]
    for e in exps[1:]:
        denom = denom + e

    chosen = sels[0]
    for sel in sels[1:]:
        chosen = jnp.logical_or(chosen, sel)
    chosen = jnp.where(chosen, 1.0, 0.0)
    before = (lax.broadcasted_iota(jnp.int32, (tm, tm), 0)
              < lax.broadcasted_iota(jnp.int32, (tm, tm), 1))
    upper = jnp.where(before, 1.0, 0.0).astype(BF16)
    rank_excl = jnp.dot(chosen.astype(BF16), upper, preferred_element_type=F32) + carry_ref[...]

    for k in range(TOP_K):
        idx_ref[k:k + 1, :] = ids[k].astype(jnp.int32)
        gate_ref[k:k + 1, :] = exps[k] / denom
        rank_ref[k:k + 1, :] = jnp.sum(jnp.where(sels[k], rank_excl, 0.0), axis=0,
                                       keepdims=True).astype(jnp.int32)

    total = carry_ref[...] + jnp.sum(chosen, axis=1, keepdims=True)
    carry_ref[...] = total
    cnt_ref[...] = jnp.broadcast_to(total, cnt_ref.shape).astype(jnp.int32)


def _router(h, g, shift, scale, w_r, b_r, seq):
    n, d = h.shape
    n_exp = w_r.shape[1]
    tm = min(512, seq)
    per_b = seq // tm
    tok = lambda dt: jax.ShapeDtypeStruct((TOP_K, n), dt)
    tok_spec = pl.BlockSpec((TOP_K, tm), lambda i: (0, i))
    return pl.pallas_call(
        _router_kernel,
        grid=(n // tm,),
        in_specs=[pl.BlockSpec((tm, d), lambda i: (i, 0)),
                  pl.BlockSpec((1, d), lambda i: (0, 0)),
                  pl.BlockSpec((None, 1, d), lambda i: (i // per_b, 0, 0)),
                  pl.BlockSpec((None, 1, d), lambda i: (i // per_b, 0, 0)),
                  pl.BlockSpec((n_exp, d), lambda i: (0, 0)),
                  pl.BlockSpec((n_exp, 1), lambda i: (0, 0))],
        out_specs=[pl.BlockSpec((tm, d // 2), lambda i: (i, 0)), tok_spec, tok_spec, tok_spec,
                   pl.BlockSpec((n_exp, LANES), lambda i: (0, 0))],
        out_shape=[jax.ShapeDtypeStruct((n, d // 2), jnp.uint32), tok(jnp.int32), tok(F32),
                   tok(jnp.int32), jax.ShapeDtypeStruct((n_exp, LANES), jnp.int32)],
        scratch_shapes=[pltpu.VMEM((n_exp, 1), F32)],
        compiler_params=_params("arbitrary"),
        name="moe_router",
    )(h, g.reshape(1, d), shift, scale, w_r.T, b_r.reshape(n_exp, 1))


def _dispatch_kernel(n_tok, tm, dest_ref, ztile_ref, np_ref, xs_ref, zeros, sem, zsem):
    i = pl.program_id(0)

    @pl.when(i == 0)
    def _():
        zeros[...] = jnp.zeros_like(zeros)
        rows = zeros.shape[0]

        def fill(t):
            return pltpu.make_async_copy(zeros, xs_ref.at[pl.ds(t * rows, rows)], zsem)

        def start(t, _):
            @pl.when(ztile_ref[t] == 1)
            def _():
                fill(t).start()
            return 0

        def finish(t, _):
            @pl.when(ztile_ref[t] == 1)
            def _():
                fill(t).wait()
            return 0

        n_tiles = xs_ref.shape[0] // rows
        lax.fori_loop(0, n_tiles, start, 0)
        lax.fori_loop(0, n_tiles, finish, 0)

    def row_copy(r, k):
        d = dest_ref[k * n_tok + i * tm + r]
        return pltpu.make_async_copy(np_ref.at[pl.ds(r, 1)], xs_ref.at[pl.ds(d, 1)], sem)

    def issue(r, _):
        for k in range(TOP_K):
            row_copy(r, k).start()
        return 0

    lax.fori_loop(0, tm, issue, 0)

    def drain(r, _):
        for k in range(TOP_K):
            row_copy(r, k).wait()
        return 0

    lax.fori_loop(0, tm, drain, 0)


def _dispatch(n_packed, dest_flat, zero_tile):
    n, half = n_packed.shape
    tm = min(256, n)
    n_rows = zero_tile.shape[0] * MOE_TILE
    return pl.pallas_call(
        functools.partial(_dispatch_kernel, n, tm),
        grid_spec=pltpu.PrefetchScalarGridSpec(
            num_scalar_prefetch=2,
            grid=(n // tm,),
            in_specs=[pl.BlockSpec((tm, half), lambda i, dest, zt: (i, 0))],
            out_specs=pl.BlockSpec(memory_space=pl.ANY),
            scratch_shapes=[pltpu.VMEM((MOE_TILE, half), jnp.uint32),
                            pltpu.SemaphoreType.DMA(()), pltpu.SemaphoreType.DMA(())]),
        out_shape=jax.ShapeDtypeStruct((n_rows, half), jnp.uint32),
        compiler_params=_params("arbitrary"),
        name="moe_dispatch",
    )(dest_flat, zero_tile, n_packed)


def _w1_prep_kernel(group, w_ref, p_ref, o_ref):
    for cb in range(w_ref.shape[1] // group):
        cols = slice(cb * group, (cb + 1) * group)
        o_ref[:, cols] = jnp.dot(w_ref[:, cols].astype(BF16), p_ref[...],
                                 preferred_element_type=F32).astype(BF16)


def _w1_prep(w1):
    shape = w1.shape
    cols = shape[-1]
    rows = math.prod(shape[:-1])
    group = min(2 * LANES, cols)
    tm = min(2048, rows)
    src = lax.broadcasted_iota(jnp.int32, (group, group), 0)
    dst = lax.broadcasted_iota(jnp.int32, (group, group), 1)
    perm = (dst == src // 2 + (group // 2) * (src % 2)).astype(BF16)
    out = pl.pallas_call(
        functools.partial(_w1_prep_kernel, group),
        grid=(rows // tm,),
        in_specs=[pl.BlockSpec((tm, cols), lambda i: (i, 0)),
                  pl.BlockSpec((group, group), lambda i: (0, 0))],
        out_specs=pl.BlockSpec((tm, cols), lambda i: (i, 0)),
        out_shape=jax.ShapeDtypeStruct((rows, cols), BF16),
        compiler_params=_params("parallel"),
        name="moe_w1_prep",
    )(w1.reshape(rows, cols), perm)
    return out.reshape(shape)


def _deinterleave_blocks(h, group):
    half = group // 2
    blocks = range(h.shape[1] // group)
    glu = jnp.concatenate([h[:, b * group:b * group + half] for b in blocks], axis=1)
    lin = jnp.concatenate([h[:, b * group + half:(b + 1) * group] for b in blocks], axis=1)
    return glu, lin


def _expert_kernel(group, texp_ref, nused_ref, first_ref, x_ref, w1_ref, w2_ref, b1g_ref, b1l_ref, b2_ref,
                   o_ref, w2b_ref):
    i = pl.program_id(0)

    @pl.when(first_ref[i] == 1)
    def _():
        w2b_ref[...] = w2_ref[...].astype(BF16)

    @pl.when(i < nused_ref[0])
    def _():
        w = x_ref[...]
        half = w.shape[1]
        x_lo = pltpu.bitcast(lax.shift_left(w, jnp.uint32(16)), F32).astype(BF16)
        x_hi = pltpu.bitcast(w & jnp.uint32(0xFFFF0000), F32).astype(BF16)
        hcat = (jnp.dot(x_lo, w1_ref[:half, :], preferred_element_type=F32)
                + jnp.dot(x_hi, w1_ref[half:, :], preferred_element_type=F32))
        glu, lin = _deinterleave_blocks(hcat, group)
        glu = jnp.minimum(glu + b1g_ref[...], SWIGLU_LIMIT)
        lin = jnp.clip(lin + b1l_ref[...], -SWIGLU_LIMIT, SWIGLU_LIMIT)
        act = glu * jax.nn.sigmoid(SWIGLU_ALPHA * glu) * (lin + 1.0)
        o_ref[...] = jnp.dot(act.astype(BF16), w2b_ref[...], preferred_element_type=F32) + b2_ref[...]

    @pl.when(i >= nused_ref[0])
    def _():
        o_ref[...] = jnp.zeros_like(o_ref)


def _experts(xs, tile_exp, n_used, first_tile, w1p, w2, layer, b1g, b1l, b2):
    n_rows, half = xs.shape
    _, n_exp, d, ff2 = w1p.shape
    ff = ff2 // 2
    n_tiles = n_rows // MOE_TILE
    tile = lambda i, te, nu, ft: (jnp.minimum(i, nu[0] - 1), 0)
    wmap = lambda i, te, nu, ft: (te[jnp.minimum(i, nu[0] - 1)], 0, 0)
    lwmap = lambda i, te, nu, ft: (layer, te[jnp.minimum(i, nu[0] - 1)], 0, 0)
    return pl.pallas_call(
        functools.partial(_expert_kernel, min(2 * LANES, ff2)),
        grid_spec=pltpu.PrefetchScalarGridSpec(
            num_scalar_prefetch=3,
            grid=(n_tiles,),
            in_specs=[pl.BlockSpec((MOE_TILE, half), tile),
                      pl.BlockSpec((None, None, d, ff2), lwmap),
                      pl.BlockSpec((None, None, ff, d), lwmap),
                      pl.BlockSpec((None, 1, ff), wmap),
                      pl.BlockSpec((None, 1, ff), wmap),
                      pl.BlockSpec((None, 1, d), wmap)],
            out_specs=pl.BlockSpec((MOE_TILE, d), lambda i, te, nu, ft: (i, 0)),
            scratch_shapes=[pltpu.VMEM((ff, d), BF16)]),
        out_shape=jax.ShapeDtypeStruct((n_rows, d), F32),
        compiler_params=_params("arbitrary"),
        name="moe_experts",
    )(tile_exp, n_used, first_tile, xs, w1p, w2, b1g, b1l, b2)


def _combine_kernel(n_tok, tm, dest_ref, yb_ref, gt_ref, h_ref, g_ref, gate_ref, o_ref, buf, sems):
    i = pl.program_id(0)
    steps = pl.num_programs(0)

    def row_copy(step, slot, r, k):
        d = dest_ref[k * n_tok + step * tm + r]
        return pltpu.make_async_copy(yb_ref.at[pl.ds(d, 1)], buf.at[slot, k, pl.ds(r, 1)], sems.at[slot])

    def issue(step, slot):
        def body(r, _):
            for k in range(TOP_K):
                row_copy(step, slot, r, k).start()
            return 0
        lax.fori_loop(0, tm, body, 0)

    slot = i % 2

    @pl.when(i == 0)
    def _():
        issue(0, 0)

    @pl.when(i + 1 < steps)
    def _():
        issue(i + 1, 1 - slot)

    def drain(r, _):
        for k in range(TOP_K):
            row_copy(i, slot, r, k).wait()
        return 0

    lax.fori_loop(0, tm, drain, 0)

    y = buf[slot, 0] * gt_ref[:, 0:1]
    for k in range(1, TOP_K):
        y = y + buf[slot, k] * gt_ref[:, k:k + 1]
    o_ref[...] = h_ref[...] + gate_ref[...] * (_rms(y) * g_ref[...])


def _combine(yb, dest_flat, gates_t, h, g, gate, seq):
    n, d = h.shape
    tm = min(128, seq)
    per_b = seq // tm
    return pl.pallas_call(
        functools.partial(_combine_kernel, n, tm),
        grid_spec=pltpu.PrefetchScalarGridSpec(
            num_scalar_prefetch=1,
            grid=(n // tm,),
            in_specs=[pl.BlockSpec(memory_space=pl.ANY),
                      pl.BlockSpec((tm, TOP_K), lambda i, dest: (i, 0)),
                      pl.BlockSpec((tm, d), lambda i, dest: (i, 0)),
                      pl.BlockSpec((1, d), lambda i, dest: (0, 0)),
                      pl.BlockSpec((None, 1, d), lambda i, dest: (i // per_b, 0, 0))],
            out_specs=pl.BlockSpec((tm, d), lambda i, dest: (i, 0)),
            scratch_shapes=[pltpu.VMEM((2, TOP_K, tm, d), F32), pltpu.SemaphoreType.DMA((2,))]),
        out_shape=jax.ShapeDtypeStruct((n, d), F32),
        compiler_params=_params("arbitrary"),
        name="moe_combine",
    )(dest_flat, yb, gates_t, h, g.reshape(1, d), gate)


def _moe_layer(h, seq, pre_g, post_g, shift, scale, gate, w_r, b_r, w1p, layer, b1, w2, b2):
    n, d = h.shape
    n_exp = w_r.shape[1]
    n_packed, idx, gates, rank, cnt = _router(h, pre_g, shift, scale, w_r, b_r, seq)

    counts = cnt[:, 0]
    tiles_e = (counts + MOE_TILE - 1) // MOE_TILE
    tile_end = jnp.cumsum(tiles_e)
    row_start = (tile_end - tiles_e) * MOE_TILE
    experts = jnp.arange(n_exp, dtype=jnp.int32)[:, None, None]
    dest = jnp.sum(jnp.where(idx[None] == experts, row_start[:, None, None], 0), axis=0) + rank
    dest = dest.reshape(-1).astype(jnp.int32)
    n_tiles = (n * TOP_K) // MOE_TILE + n_exp
    tile_exp = jnp.sum(tile_end[None, :] <= jnp.arange(n_tiles)[:, None], axis=1)
    tile_exp = jnp.minimum(tile_exp, n_exp - 1).astype(jnp.int32)
    n_used = tile_end[-1:].astype(jnp.int32)
    tile_ids = jnp.arange(n_tiles)
    last_of_expert = jnp.any((tile_ids[:, None] == tile_end[None, :] - 1) & (tiles_e[None, :] > 0), axis=1)
    zero_tile = (last_of_expert | (tile_ids >= n_used[0])).astype(jnp.int32)
    first_tile = jnp.any((tile_ids[:, None] == (tile_end - tiles_e)[None, :]) & (tiles_e[None, :] > 0),
                         axis=1).astype(jnp.int32)

    xs = _dispatch(n_packed, dest, zero_tile)
    ff = w2.shape[2]
    yb = _experts(xs, tile_exp, n_used, first_tile, w1p, w2, layer,
                  b1[:, 0::2].reshape(n_exp, 1, ff), b1[:, 1::2].reshape(n_exp, 1, ff),
                  b2.reshape(n_exp, 1, d))
    return _combine(yb, dest, gates.T, h, post_g, gate, seq)


def _lambda_init(layer):
    return 0.8 - 0.6 * math.exp(-0.3 * layer)


def kernel(x, c, mix_pre_g, mix_post_g, mix_mod_w, mix_mod_b, ffn_pre_g, ffn_post_g, ffn_mod_w, ffn_mod_b, a_w_in, a_ln_g, a_ln_b, a_w_s, a_b_s, a_w_out, kv_g, kv_mod_w, kv_mod_b, w_k, w_v, b_w_q, b_lq1, b_lk1, b_lq2, b_lk2, b_subln_g, b_w_o, moe_w_r, moe_b_r, moe_w1, moe_b1, moe_w2, moe_b2):
    batch, seq, d = x.shape
    depth = mix_pre_g.shape[0]
    n_a = a_w_in.shape[0]
    dh = b_lq1.shape[-1]
    h = x.reshape(batch * seq, d)
    c_pad = jnp.zeros((16, d), F32).at[:batch].set(c)
    moe_w1p = _w1_prep(moe_w1)

    kv = None
    for l in range(depth):
        shift, scale, gate = _modulation(c_pad, mix_mod_w, mix_mod_b, l, 3, batch)
        if l < n_a:
            (n,) = _modnorm(h, [(mix_pre_g[l], shift, scale)], seq)
            z = _matmul(n, a_w_in[l].astype(BF16), BF16, epilogue="gelu", name="gmlp_in")
            gated = _gmlp_gate(z, a_ln_g[l], a_ln_b[l], a_w_s[l], a_b_s[l])
            out = _matmul(gated, a_w_out[l].astype(BF16), F32, name="gmlp_out")
        else:
            j = l - n_a
            if kv is None:
                kv_shift, kv_scale = _modulation(c_pad, kv_mod_w[None], kv_mod_b[None], 0, 2, batch)
                n_kv, n = _modnorm(h, [(kv_g, kv_shift, kv_scale), (mix_pre_g[l], shift, scale)], seq)
                w_kv = jnp.concatenate([w_k, w_v], axis=1).astype(BF16)
                kv = _matmul(n_kv, w_kv, BF16, name="kv_proj").reshape(batch, seq, -1)
            else:
                (n,) = _modnorm(h, [(mix_pre_g[l], shift, scale)], seq)
            q = _matmul(n, b_w_q[j].astype(BF16), BF16, epilogue="scale",
                        scale=LOG2E * dh ** -0.5, name="q_proj").reshape(batch, seq, -1)
            o = _diff_attention(q, kv, b_lq1[j], b_lk1[j], b_lq2[j], b_lk2[j], b_subln_g[j],
                                _lambda_init(l))
            out = _matmul(o.reshape(batch * seq, -1), b_w_o[j].astype(BF16), F32, name="attn_out")
        h = _post(h, out, mix_post_g[l], gate, seq)

        shift, scale, gate = _modulation(c_pad, ffn_mod_w, ffn_mod_b, l, 3, batch)
        h = _moe_layer(h, seq, ffn_pre_g[l], ffn_post_g[l], shift, scale, gate,
                       moe_w_r[l], moe_b_r[l], moe_w1p, l, moe_b1[l], moe_w2, moe_b2[l])
    return h.reshape(batch, seq, d)
```

```python
import functools
import math

import jax
import jax.numpy as jnp
from jax import lax
from jax.experimental import pallas as pl
from jax.experimental.pallas import tpu as pltpu

RMS_EPS = 1e-6
LN_EPS = 1e-5
TOP_K = 4
SWIGLU_ALPHA = 1.702
SWIGLU_LIMIT = 7.0
LOG2E = 1.4426950408889634

VMEM_LIMIT_BYTES = 56 * 1024 * 1024
LANES = 128
MOE_TILE = 256

BF16 = jnp.bfloat16
F32 = jnp.float32


def _params(*sem):
    return pltpu.CompilerParams(dimension_semantics=sem, vmem_limit_bytes=VMEM_LIMIT_BYTES)


def _rms(x):
    return x * lax.rsqrt(jnp.mean(x * x, axis=-1, keepdims=True) + RMS_EPS)


def _mod_kernel(c_ref, w_ref, b_ref, o_ref):
    c = c_ref[...]
    s = c * jax.nn.sigmoid(c)
    s_hi = s.astype(BF16)
    s_lo = (s - s_hi.astype(F32)).astype(BF16)
    lhs = jnp.concatenate([s_hi, s_lo], axis=0)
    r = jnp.dot(lhs, w_ref[...].astype(BF16), preferred_element_type=F32)
    rows = c.shape[0]
    o_ref[...] = r[:rows] + r[rows:] + b_ref[...]


def _modulation(c_pad, w3, b2, layer, n_split, batch):
    rows, d = c_pad.shape
    nout = w3.shape[-1]
    tn = min(512, nout)
    out = pl.pallas_call(
        _mod_kernel,
        grid=(nout // tn,),
        in_specs=[
            pl.BlockSpec((rows, d), lambda j: (0, 0)),
            pl.BlockSpec((None, d, tn), lambda j: (layer, 0, j)),
            pl.BlockSpec((None, 1, tn), lambda j: (layer, 0, j)),
        ],
        out_specs=pl.BlockSpec((rows, tn), lambda j: (0, j)),
        out_shape=jax.ShapeDtypeStruct((rows, nout), F32),
        compiler_params=_params("arbitrary"),
        name="modulation",
    )(c_pad, w3, b2.reshape(b2.shape[0], 1, nout))
    return [t[:batch, None, :] for t in jnp.split(out, n_split, axis=-1)]


def _modnorm_kernel(n_out, h_ref, *refs):
    y = _rms(h_ref[...])
    for t in range(n_out):
        g_ref, sh_ref, sc_ref = refs[3 * t:3 * t + 3]
        o_ref = refs[3 * n_out + t]
        o_ref[...] = ((y * g_ref[...]) * (1.0 + sc_ref[...]) + sh_ref[...]).astype(o_ref.dtype)


def _modnorm(h, mods, seq):
    n, d = h.shape
    tm = min(512, seq)
    per_b = seq // tm
    ins, specs = [h], [pl.BlockSpec((tm, d), lambda i: (i, 0))]
    for g, sh, sc in mods:
        ins += [g.reshape(1, d), sh, sc]
        specs += [pl.BlockSpec((1, d), lambda i: (0, 0)),
                  pl.BlockSpec((None, 1, d), lambda i: (i // per_b, 0, 0)),
                  pl.BlockSpec((None, 1, d), lambda i: (i // per_b, 0, 0))]
    outs = pl.pallas_call(
        functools.partial(_modnorm_kernel, len(mods)),
        grid=(n // tm,),
        in_specs=specs,
        out_specs=[pl.BlockSpec((tm, d), lambda i: (i, 0)) for _ in mods],
        out_shape=[jax.ShapeDtypeStruct((n, d), BF16) for _ in mods],
        compiler_params=_params("parallel"),
        name="modnorm",
    )(*ins)
    return list(outs)


def _matmul_kernel(epilogue, scale, x_ref, w_ref, o_ref):
    acc = jnp.dot(x_ref[...], w_ref[...], preferred_element_type=F32)
    if epilogue == "gelu":
        acc = 0.5 * acc * (1.0 + lax.erf(acc * (1.0 / math.sqrt(2.0))))
    elif epilogue == "scale":
        acc = acc * scale
    o_ref[...] = acc.astype(o_ref.dtype)


def _matmul(x, w, out_dtype, epilogue=None, scale=1.0, name="matmul"):
    m, k = x.shape
    nn = w.shape[1]
    tm = min(1024, m)
    tn = min(1024 if out_dtype == BF16 else 512, nn)
    return pl.pallas_call(
        functools.partial(_matmul_kernel, epilogue, scale),
        grid=(m // tm, nn // tn),
        in_specs=[pl.BlockSpec((tm, k), lambda i, j: (i, 0)),
                  pl.BlockSpec((k, tn), lambda i, j: (0, j))],
        out_specs=pl.BlockSpec((tm, tn), lambda i, j: (i, j)),
        out_shape=jax.ShapeDtypeStruct((m, nn), out_dtype),
        compiler_params=_params("parallel", "arbitrary"),
        name=name,
    )(x, w)


def _gate_kernel(chunk, groups, u_ref, v_ref, g_ref, b_ref, ws_ref, bs_ref, o_ref):
    v = v_ref[...].astype(F32)
    mu = jnp.mean(v, axis=-1, keepdims=True)
    vc = v - mu
    var = jnp.mean(vc * vc, axis=-1, keepdims=True)
    vln = (vc * lax.rsqrt(var + LN_EPS) * g_ref[...] + b_ref[...]).astype(BF16)
    tm, width = vln.shape
    gd = width // groups
    causal = (lax.broadcasted_iota(jnp.int32, (chunk, chunk), 0)
              >= lax.broadcasted_iota(jnp.int32, (chunk, chunk), 1))
    for g in range(groups):
        ws = jnp.where(causal, ws_ref[g], 0.0).astype(BF16)
        bias = bs_ref[:, g:g + 1]
        for c in range(tm // chunk):
            rows = slice(c * chunk, (c + 1) * chunk)
            cols = slice(g * gd, (g + 1) * gd)
            sv = jnp.dot(ws, vln[rows, cols], preferred_element_type=F32) + bias
            o_ref[rows, cols] = (u_ref[rows, cols].astype(F32) * sv).astype(o_ref.dtype)


def _gmlp_gate(z, ln_g, ln_b, w_s, b_s):
    n, w2 = z.shape
    width = w2 // 2
    groups, chunk, _ = w_s.shape
    tm = min(2 * chunk, n)
    return pl.pallas_call(
        functools.partial(_gate_kernel, chunk, groups),
        grid=(n // tm,),
        in_specs=[pl.BlockSpec((tm, width), lambda i: (i, 0)),
                  pl.BlockSpec((tm, width), lambda i: (i, 1)),
                  pl.BlockSpec((1, width), lambda i: (0, 0)),
                  pl.BlockSpec((1, width), lambda i: (0, 0)),
                  pl.BlockSpec((groups, chunk, chunk), lambda i: (0, 0, 0)),
                  pl.BlockSpec((chunk, groups), lambda i: (0, 0))],
        out_specs=pl.BlockSpec((tm, width), lambda i: (i, 0)),
        out_shape=jax.ShapeDtypeStruct((n, width), BF16),
        compiler_params=_params("parallel"),
        name="gmlp_gate",
    )(z, z, ln_g.reshape(1, width), ln_b.reshape(1, width), w_s, b_s.T)


ATTN_ROW_CHUNK = 256


def _attn_kernel(tq, tk, dh, lambda_init, q_ref, k_ref, v_ref, lq1_ref, lk1_ref, lq2_ref, lk2_ref,
                 sg_ref, o_ref, *scratch):
    s_ref = (scratch[0:2], scratch[2:4])
    p_ref = (scratch[4:6], scratch[6:8])
    al_ref = (scratch[8:10], scratch[10:12])
    m_ref, l_ref, acc_ref = scratch[12:14], scratch[14:16], scratch[16:18]
    qi = pl.program_id(2)
    dims = (((1,), (1,)), ((), ()))
    rc = min(ATTN_ROW_CHUNK, tq)

    def live_rows(diag):
        return slice(tk, tq) if diag == 1 else slice(0, tq)

    def scores(blk, buf, diag=None):
        start = pl.multiple_of(blk * tk, tk)
        rows = live_rows(diag)
        for mp in range(2):
            s_ref[buf][mp][rows, :] = lax.dot_general(
                q_ref[rows, mp * dh:(mp + 1) * dh], k_ref[pl.ds(start, tk), pl.ds(mp * dh, dh)], dims,
                preferred_element_type=F32)

    def accumulate(blk, buf, diag=None):
        start = pl.multiple_of(blk * tk, tk)
        rows = live_rows(diag)
        v = v_ref[pl.ds(start, tk), :]
        for mp in range(2):
            alpha = jnp.tile(al_ref[buf][mp][rows, :], (1, 2 * dh // LANES))
            acc_ref[mp][rows, :] = alpha * acc_ref[mp][rows, :] + jnp.dot(
                p_ref[buf][mp][rows, :], v, preferred_element_type=F32)

    def softmax(buf, diag=None):
        nblk = tk // LANES
        first = 0 if diag != 1 else tk // rc
        for mp in range(2):
            for c in range(first, tq // rc):
                rows = slice(c * rc, (c + 1) * rc)
                sb = []
                for j in range(nblk):
                    col_lo = (diag or 0) * tk + j * LANES
                    if diag is not None and col_lo > c * rc + rc - 1:
                        sb.append(None)
                        continue
                    s = s_ref[buf][mp][rows, j * LANES:(j + 1) * LANES]
                    if diag is not None and col_lo + LANES - 1 > c * rc:
                        keep = (lax.broadcasted_iota(jnp.int32, (rc, LANES), 1) + (col_lo - c * rc)
                                <= lax.broadcasted_iota(jnp.int32, (rc, LANES), 0))
                        s = jnp.where(keep, s, -jnp.inf)
                    sb.append(s)
                live = [s for s in sb if s is not None]
                part = live[0]
                for s in live[1:]:
                    part = jnp.maximum(part, s)
                m_old = m_ref[mp][rows, :]
                m_new = jnp.maximum(m_old, jnp.max(part, axis=-1, keepdims=True))
                pb = [None if s is None else jnp.exp2(s - m_new) for s in sb]
                live = [p for p in pb if p is not None]
                part = live[0]
                for p in live[1:]:
                    part = part + p
                alpha = jnp.exp2(m_old - m_new)
                l_ref[mp][rows, :] = alpha * l_ref[mp][rows, :] + jnp.sum(part, axis=-1, keepdims=True)
                m_ref[mp][rows, :] = m_new
                al_ref[buf][mp][rows, :] = alpha
                for j in range(nblk):
                    p_ref[buf][mp][rows, j * LANES:(j + 1) * LANES] = (
                        jnp.zeros((rc, LANES), BF16) if pb[j] is None else pb[j].astype(BF16))

    for mp in range(2):
        m_ref[mp][...] = jnp.full(m_ref[mp].shape, -jnp.inf, F32)
        l_ref[mp][...] = jnp.zeros(l_ref[mp].shape, F32)
        acc_ref[mp][...] = jnp.zeros(acc_ref[mp].shape, F32)
        p_ref[1][mp][...] = jnp.zeros(p_ref[1][mp].shape, BF16)
        al_ref[1][mp][...] = jnp.ones(al_ref[1][mp].shape, F32)
    scores(0, 0)

    def pair(t, _):
        scores(2 * t + 1, 1)
        accumulate(jnp.maximum(2 * t - 1, 0), 1)
        softmax(0)
        scores(2 * t + 2, 0)
        accumulate(2 * t, 0)
        softmax(1)
        return 0

    lax.fori_loop(0, qi, pair, 0)
    scores(2 * qi + 1, 1, diag=1)
    accumulate(jnp.maximum(2 * qi - 1, 0), 1)
    softmax(0, diag=0)
    accumulate(2 * qi, 0)
    softmax(1, diag=1)
    accumulate(2 * qi + 1, 1, diag=1)

    lam = (jnp.exp(jnp.sum(lq1_ref[...] * lk1_ref[...], axis=-1, keepdims=True))
           - jnp.exp(jnp.sum(lq2_ref[...] * lk2_ref[...], axis=-1, keepdims=True)) + lambda_init)
    inv1 = 1.0 / l_ref[0][...]
    inv2 = lam / l_ref[1][...]
    o = jnp.concatenate(
        [acc_ref[0][:, j * LANES:(j + 1) * LANES] * inv1 - acc_ref[1][:, j * LANES:(j + 1) * LANES] * inv2
         for j in range(2 * dh // LANES)], axis=1)
    o = _rms(o) * sg_ref[...] * (1.0 - lambda_init)
    o_ref[...] = o.astype(o_ref.dtype)


def _diff_attention(q, kv, lq1, lk1, lq2, lk2, subln_g, lambda_init):
    b, s, aw = q.shape
    dh = lq1.shape[-1]
    heads = aw // (2 * dh)
    tq = min(1024, s)
    tk = tq // 2
    vec = lambda a: a.reshape(1, -1)
    small = lambda w: pl.BlockSpec((1, w), lambda bi, hi, qi: (0, 0))
    return pl.pallas_call(
        functools.partial(_attn_kernel, tq, tk, dh, lambda_init),
        grid=(b, heads, s // tq),
        in_specs=[pl.BlockSpec((None, tq, 2 * dh), lambda bi, hi, qi: (bi, qi, hi)),
                  pl.BlockSpec((None, s, 2 * dh), lambda bi, hi, qi: (bi, 0, hi)),
                  pl.BlockSpec((None, s, 2 * dh), lambda bi, hi, qi: (bi, 0, heads + hi)),
                  small(dh), small(dh), small(dh), small(dh), small(2 * dh)],
        out_specs=pl.BlockSpec((None, tq, 2 * dh), lambda bi, hi, qi: (bi, qi, hi)),
        out_shape=jax.ShapeDtypeStruct((b, s, aw), BF16),
        scratch_shapes=([pltpu.VMEM((tq, tk), F32)] * 4 + [pltpu.VMEM((tq, tk), BF16)] * 4
                        + [pltpu.VMEM((tq, LANES), F32)] * 8 + [pltpu.VMEM((tq, 2 * dh), F32)] * 2),
        compiler_params=_params("parallel", "parallel", "arbitrary"),
        name="diff_attention",
    )(q, kv, kv, vec(lq1), vec(lk1), vec(lq2), vec(lk2), vec(subln_g))


def _router_kernel(h_ref, y_ref, pg_ref, pgate_ref, g_ref, sh_ref, sc_ref, wr_ref, br_ref,
                   h1_ref, np_ref, idx_ref, gate_ref, rank_ref, cnt_ref, carry_ref):
    i = pl.program_id(0)

    @pl.when(i == 0)
    def _():
        carry_ref[...] = jnp.zeros_like(carry_ref)

    h1 = h_ref[...] + pgate_ref[...] * (_rms(y_ref[...]) * pg_ref[...])
    h1_ref[...] = h1
    n = (_rms(h1) * g_ref[...]) * (1.0 + sc_ref[...]) + sh_ref[...]
    tm, d = n.shape
    n_hi = n.astype(BF16)
    n_hi32 = n_hi.astype(F32)
    n_lo = (n - n_hi32).astype(BF16)

    bits = pltpu.bitcast(n_hi32, jnp.uint32)
    half = d // 2
    np_ref[...] = (lax.shift_right_logical(bits[:, :half], jnp.uint32(16))
                   | (bits[:, half:] & jnp.uint32(0xFFFF0000)))

    wr = wr_ref[...]
    w_hi = wr.astype(BF16)
    w_lo = (wr - w_hi.astype(F32)).astype(BF16)
    dims = (((1,), (1,)), ((), ()))
    logits = (lax.dot_general(w_hi, n_hi, dims, preferred_element_type=F32)
              + lax.dot_general(w_lo, n_hi, dims, preferred_element_type=F32)
              + lax.dot_general(w_hi, n_lo, dims, preferred_element_type=F32)
              + br_ref[...])
    n_exp = logits.shape[0]
    eid = lax.broadcasted_iota(jnp.int32, (n_exp, tm), 0).astype(F32)

    vals, sels, ids = [], [], []
    rem = logits
    for _ in range(TOP_K):
        m = jnp.max(rem, axis=0, keepdims=True)
        first = jnp.min(jnp.where(rem == m, eid, float(n_exp)), axis=0, keepdims=True)
        sel = eid == first
        rem = jnp.where(sel, -jnp.inf, rem)
        vals.append(m)
        sels.append(sel)
        ids.append(first)

    exps = [jnp.exp(v - vals[0]) for v in vals]
    denom = exps[0]
    for e in exps[1:]:
        denom = denom + e

    chosen = sels[0]
    for sel in sels[1:]:
        chosen = jnp.logical_or(chosen, sel)
    chosen = jnp.where(chosen, 1.0, 0.0)
    before = (lax.broadcasted_iota(jnp.int32, (tm, tm), 0)
              < lax.broadcasted_iota(jnp.int32, (tm, tm), 1))
    upper = jnp.where(before, 1.0, 0.0).astype(BF16)
    rank_excl = jnp.dot(chosen.astype(BF16), upper, preferred_element_type=F32) + carry_ref[...]

    for k in range(TOP_K):
        idx_ref[k:k + 1, :] = ids[k].astype(jnp.int32)
        gate_ref[k:k + 1, :] = exps[k] / denom
        rank_ref[k:k + 1, :] = jnp.sum(jnp.where(sels[k], rank_excl, 0.0), axis=0,
                                       keepdims=True).astype(jnp.int32)

    total = carry_ref[...] + jnp.sum(chosen, axis=1, keepdims=True)
    carry_ref[...] = total
    cnt_ref[...] = jnp.broadcast_to(total, cnt_ref.shape).astype(jnp.int32)


def _router(h, y, post_g, post_gate, g, shift, scale, w_r, b_r, seq):
    n, d = h.shape
    n_exp = w_r.shape[1]
    tm = min(256, seq)
    per_b = seq // tm
    tok = lambda dt: jax.ShapeDtypeStruct((TOP_K, n), dt)
    tok_spec = pl.BlockSpec((TOP_K, tm), lambda i: (0, i))
    row_spec = pl.BlockSpec((tm, d), lambda i: (i, 0))
    vec_spec = pl.BlockSpec((1, d), lambda i: (0, 0))
    mod_spec = pl.BlockSpec((None, 1, d), lambda i: (i // per_b, 0, 0))
    return pl.pallas_call(
        _router_kernel,
        grid=(n // tm,),
        in_specs=[row_spec, row_spec, vec_spec, mod_spec, vec_spec, mod_spec, mod_spec,
                  pl.BlockSpec((n_exp, d), lambda i: (0, 0)),
                  pl.BlockSpec((n_exp, 1), lambda i: (0, 0))],
        out_specs=[row_spec, pl.BlockSpec((tm, d // 2), lambda i: (i, 0)), tok_spec, tok_spec, tok_spec,
                   pl.BlockSpec((n_exp, LANES), lambda i: (0, 0))],
        out_shape=[jax.ShapeDtypeStruct((n, d), F32), jax.ShapeDtypeStruct((n, d // 2), jnp.uint32),
                   tok(jnp.int32), tok(F32), tok(jnp.int32),
                   jax.ShapeDtypeStruct((n_exp, LANES), jnp.int32)],
        scratch_shapes=[pltpu.VMEM((n_exp, 1), F32)],
        compiler_params=_params("arbitrary"),
        name="moe_router",
    )(h, y, post_g.reshape(1, d), post_gate, g.reshape(1, d), shift, scale, w_r.T,
      b_r.reshape(n_exp, 1))


def _dispatch_kernel(n_tok, tm, dest_ref, ztile_ref, np_ref, xs_ref, zeros, sem, zsem):
    i = pl.program_id(0)

    @pl.when(i == 0)
    def _():
        zeros[...] = jnp.zeros_like(zeros)
        rows = zeros.shape[0]

        def fill(t):
            return pltpu.make_async_copy(zeros, xs_ref.at[pl.ds(t * rows, rows)], zsem)

        def start(t, _):
            @pl.when(ztile_ref[t] == 1)
            def _():
                fill(t).start()
            return 0

        def finish(t, _):
            @pl.when(ztile_ref[t] == 1)
            def _():
                fill(t).wait()
            return 0

        n_tiles = xs_ref.shape[0] // rows
        lax.fori_loop(0, n_tiles, start, 0)
        lax.fori_loop(0, n_tiles, finish, 0)

    def row_copy(r, k):
        d = dest_ref[k * n_tok + i * tm + r]
        return pltpu.make_async_copy(np_ref.at[pl.ds(r, 1)], xs_ref.at[pl.ds(d, 1)], sem)

    def issue(r, _):
        for k in range(TOP_K):
            row_copy(r, k).start()
        return 0

    lax.fori_loop(0, tm, issue, 0)

    def drain(r, _):
        for k in range(TOP_K):
            row_copy(r, k).wait()
        return 0

    lax.fori_loop(0, tm, drain, 0)


def _dispatch(n_packed, dest_flat, zero_tile):
    n, half = n_packed.shape
    tm = min(256, n)
    n_rows = zero_tile.shape[0] * MOE_TILE
    return pl.pallas_call(
        functools.partial(_dispatch_kernel, n, tm),
        grid_spec=pltpu.PrefetchScalarGridSpec(
            num_scalar_prefetch=2,
            grid=(n // tm,),
            in_specs=[pl.BlockSpec((tm, half), lambda i, dest, zt: (i, 0))],
            out_specs=pl.BlockSpec(memory_space=pl.ANY),
            scratch_shapes=[pltpu.VMEM((MOE_TILE, half), jnp.uint32),
                            pltpu.SemaphoreType.DMA(()), pltpu.SemaphoreType.DMA(())]),
        out_shape=jax.ShapeDtypeStruct((n_rows, half), jnp.uint32),
        compiler_params=_params("arbitrary"),
        name="moe_dispatch",
    )(dest_flat, zero_tile, n_packed)


def _w1_prep_kernel(group, w_ref, p_ref, o_ref):
    for cb in range(w_ref.shape[1] // group):
        cols = slice(cb * group, (cb + 1) * group)
        o_ref[:, cols] = jnp.dot(w_ref[:, cols].astype(BF16), p_ref[...],
                                 preferred_element_type=F32).astype(BF16)


def _w1_prep(w1):
    shape = w1.shape
    cols = shape[-1]
    rows = math.prod(shape[:-1])
    group = min(2 * LANES, cols)
    tm = min(2048, rows)
    src = lax.broadcasted_iota(jnp.int32, (group, group), 0)
    dst = lax.broadcasted_iota(jnp.int32, (group, group), 1)
    perm = (dst == src // 2 + (group // 2) * (src % 2)).astype(BF16)
    out = pl.pallas_call(
        functools.partial(_w1_prep_kernel, group),
        grid=(rows // tm,),
        in_specs=[pl.BlockSpec((tm, cols), lambda i: (i, 0)),
                  pl.BlockSpec((group, group), lambda i: (0, 0))],
        out_specs=pl.BlockSpec((tm, cols), lambda i: (i, 0)),
        out_shape=jax.ShapeDtypeStruct((rows, cols), BF16),
        compiler_params=_params("parallel"),
        name="moe_w1_prep",
    )(w1.reshape(rows, cols), perm)
    return out.reshape(shape)


def _deinterleave_blocks(h, group):
    half = group // 2
    blocks = range(h.shape[1] // group)
    glu = jnp.concatenate([h[:, b * group:b * group + half] for b in blocks], axis=1)
    lin = jnp.concatenate([h[:, b * group + half:(b + 1) * group] for b in blocks], axis=1)
    return glu, lin


def _expert_kernel(group, texp_ref, nused_ref, first_ref, x_ref, w1_ref, w2_ref, b1g_ref, b1l_ref, b2_ref,
                   o_ref, w2b_ref):
    i = pl.program_id(0)

    @pl.when(first_ref[i] == 1)
    def _():
        w2b_ref[...] = w2_ref[...].astype(BF16)

    @pl.when(i < nused_ref[0])
    def _():
        w = x_ref[...]
        half = w.shape[1]
        x_lo = pltpu.bitcast(lax.shift_left(w, jnp.uint32(16)), F32).astype(BF16)
        x_hi = pltpu.bitcast(w & jnp.uint32(0xFFFF0000), F32).astype(BF16)
        hcat = (jnp.dot(x_lo, w1_ref[:half, :], preferred_element_type=F32)
                + jnp.dot(x_hi, w1_ref[half:, :], preferred_element_type=F32))
        glu, lin = _deinterleave_blocks(hcat, group)
        glu = jnp.minimum(glu + b1g_ref[...], SWIGLU_LIMIT)
        lin = jnp.clip(lin + b1l_ref[...], -SWIGLU_LIMIT, SWIGLU_LIMIT)
        act = glu * jax.nn.sigmoid(SWIGLU_ALPHA * glu) * (lin + 1.0)
        o_ref[...] = jnp.dot(act.astype(BF16), w2b_ref[...], preferred_element_type=F32) + b2_ref[...]

    @pl.when(i >= nused_ref[0])
    def _():
        o_ref[...] = jnp.zeros_like(o_ref)


def _experts(xs, tile_exp, n_used, first_tile, w1p, w2, layer, b1g, b1l, b2):
    n_rows, half = xs.shape
    _, n_exp, d, ff2 = w1p.shape
    ff = ff2 // 2
    n_tiles = n_rows // MOE_TILE
    tile = lambda i, te, nu, ft: (jnp.minimum(i, nu[0] - 1), 0)
    wmap = lambda i, te, nu, ft: (te[jnp.minimum(i, nu[0] - 1)], 0, 0)
    lwmap = lambda i, te, nu, ft: (layer, te[jnp.minimum(i, nu[0] - 1)], 0, 0)
    return pl.pallas_call(
        functools.partial(_expert_kernel, min(2 * LANES, ff2)),
        grid_spec=pltpu.PrefetchScalarGridSpec(
            num_scalar_prefetch=3,
            grid=(n_tiles,),
            in_specs=[pl.BlockSpec((MOE_TILE, half), tile),
                      pl.BlockSpec((None, None, d, ff2), lwmap),
                      pl.BlockSpec((None, None, ff, d), lwmap),
                      pl.BlockSpec((None, 1, ff), wmap),
                      pl.BlockSpec((None, 1, ff), wmap),
                      pl.BlockSpec((None, 1, d), wmap)],
            out_specs=pl.BlockSpec((MOE_TILE, d), lambda i, te, nu, ft: (i, 0)),
            scratch_shapes=[pltpu.VMEM((ff, d), BF16)]),
        out_shape=jax.ShapeDtypeStruct((n_rows, d), F32),
        compiler_params=_params("arbitrary"),
        name="moe_experts",
    )(tile_exp, n_used, first_tile, xs, w1p, w2, b1g, b1l, b2)


def _combine_kernel(n_tok, tm, n_mods, dest_ref, yb_ref, gt_ref, h_ref, g_ref, gate_ref, *refs):
    mod_refs = refs[:3 * n_mods]
    o_ref = refs[3 * n_mods]
    n_refs = refs[3 * n_mods + 1:4 * n_mods + 1]
    buf, sems = refs[4 * n_mods + 1:]
    i = pl.program_id(0)
    steps = pl.num_programs(0)

    def row_copy(step, slot, r, k):
        d = dest_ref[k * n_tok + step * tm + r]
        return pltpu.make_async_copy(yb_ref.at[pl.ds(d, 1)], buf.at[slot, k, pl.ds(r, 1)], sems.at[slot])

    def issue(step, slot):
        def body(r, _):
            for k in range(TOP_K):
                row_copy(step, slot, r, k).start()
            return 0
        lax.fori_loop(0, tm, body, 0)

    slot = i % 2

    @pl.when(i == 0)
    def _():
        issue(0, 0)

    @pl.when(i + 1 < steps)
    def _():
        issue(i + 1, 1 - slot)

    def drain(r, _):
        for k in range(TOP_K):
            row_copy(i, slot, r, k).wait()
        return 0

    lax.fori_loop(0, tm, drain, 0)

    y = buf[slot, 0] * gt_ref[:, 0:1]
    for k in range(1, TOP_K):
        y = y + buf[slot, k] * gt_ref[:, k:k + 1]
    h_new = h_ref[...] + gate_ref[...] * (_rms(y) * g_ref[...])
    o_ref[...] = h_new
    if n_mods:
        yn = _rms(h_new)
        for t in range(n_mods):
            mg_ref, sh_ref, sc_ref = mod_refs[3 * t:3 * t + 3]
            n_refs[t][...] = ((yn * mg_ref[...]) * (1.0 + sc_ref[...]) + sh_ref[...]).astype(BF16)


def _combine(yb, dest_flat, gates_t, h, g, gate, seq, next_mods):
    n, d = h.shape
    tm = min(128, seq)
    per_b = seq // tm
    row_spec = pl.BlockSpec((tm, d), lambda i, dest: (i, 0))
    vec_spec = pl.BlockSpec((1, d), lambda i, dest: (0, 0))
    mod_spec = pl.BlockSpec((None, 1, d), lambda i, dest: (i // per_b, 0, 0))
    mod_ins, mod_specs = [], []
    for mg, sh, sc in next_mods:
        mod_ins += [mg.reshape(1, d), sh, sc]
        mod_specs += [vec_spec, mod_spec, mod_spec]
    outs = pl.pallas_call(
        functools.partial(_combine_kernel, n, tm, len(next_mods)),
        grid_spec=pltpu.PrefetchScalarGridSpec(
            num_scalar_prefetch=1,
            grid=(n // tm,),
            in_specs=[pl.BlockSpec(memory_space=pl.ANY),
                      pl.BlockSpec((tm, TOP_K), lambda i, dest: (i, 0)),
                      row_spec, vec_spec, mod_spec] + mod_specs,
            out_specs=[row_spec] * (1 + len(next_mods)),
            scratch_shapes=[pltpu.VMEM((2, TOP_K, tm, d), F32), pltpu.SemaphoreType.DMA((2,))]),
        out_shape=[jax.ShapeDtypeStruct((n, d), F32)]
        + [jax.ShapeDtypeStruct((n, d), BF16) for _ in next_mods],
        compiler_params=_params("arbitrary"),
        name="moe_combine",
    )(dest_flat, yb, gates_t, h, g.reshape(1, d), gate, *mod_ins)
    return outs[0], list(outs[1:])


def _moe_layer(h, mix_out, mix_post_g, mix_gate, seq, pre_g, post_g, shift, scale, gate, w_r, b_r, w1p,
               layer, b1, w2, b2, next_mods):
    n, d = h.shape
    n_exp = w_r.shape[1]
    h, n_packed, idx, gates, rank, cnt = _router(h, mix_out, mix_post_g, mix_gate, pre_g, shift, scale,
                                                 w_r, b_r, seq)

    counts = cnt[:, 0]
    tiles_e = (counts + MOE_TILE - 1) // MOE_TILE
    tile_end = jnp.cumsum(tiles_e)
    row_start = (tile_end - tiles_e) * MOE_TILE
    experts = jnp.arange(n_exp, dtype=jnp.int32)[:, None, None]
    dest = jnp.sum(jnp.where(idx[None] == experts, row_start[:, None, None], 0), axis=0) + rank
    dest = dest.reshape(-1).astype(jnp.int32)
    n_tiles = (n * TOP_K) // MOE_TILE + n_exp
    tile_exp = jnp.sum(tile_end[None, :] <= jnp.arange(n_tiles)[:, None], axis=1)
    tile_exp = jnp.minimum(tile_exp, n_exp - 1).astype(jnp.int32)
    n_used = tile_end[-1:].astype(jnp.int32)
    tile_ids = jnp.arange(n_tiles)
    last_of_expert = jnp.any((tile_ids[:, None] == tile_end[None, :] - 1) & (tiles_e[None, :] > 0), axis=1)
    zero_tile = (last_of_expert | (tile_ids >= n_used[0])).astype(jnp.int32)
    first_tile = jnp.any((tile_ids[:, None] == (tile_end - tiles_e)[None, :]) & (tiles_e[None, :] > 0),
                         axis=1).astype(jnp.int32)

    xs = _dispatch(n_packed, dest, zero_tile)
    ff = w2.shape[2]
    yb = _experts(xs, tile_exp, n_used, first_tile, w1p, w2, layer,
                  b1[:, 0::2].reshape(n_exp, 1, ff), b1[:, 1::2].reshape(n_exp, 1, ff),
                  b2.reshape(n_exp, 1, d))
    return _combine(yb, dest, gates.T, h, post_g, gate, seq, next_mods)


def _lambda_init(layer):
    return 0.8 - 0.6 * math.exp(-0.3 * layer)


def kernel(x, c, mix_pre_g, mix_post_g, mix_mod_w, mix_mod_b, ffn_pre_g, ffn_post_g, ffn_mod_w, ffn_mod_b, a_w_in, a_ln_g, a_ln_b, a_w_s, a_b_s, a_w_out, kv_g, kv_mod_w, kv_mod_b, w_k, w_v, b_w_q, b_lq1, b_lk1, b_lq2, b_lk2, b_subln_g, b_w_o, moe_w_r, moe_b_r, moe_w1, moe_b1, moe_w2, moe_b2):
    batch, seq, d = x.shape
    depth = mix_pre_g.shape[0]
    n_a = a_w_in.shape[0]
    dh = b_lq1.shape[-1]
    h = x.reshape(batch * seq, d)
    c_pad = jnp.zeros((16, d), F32).at[:batch].set(c)
    moe_w1p = _w1_prep(moe_w1)

    mix_mods = [_modulation(c_pad, mix_mod_w, mix_mod_b, l, 3, batch) for l in range(depth)]
    ffn_mods = [_modulation(c_pad, ffn_mod_w, ffn_mod_b, l, 3, batch) for l in range(depth)]
    if depth > n_a:
        kv_shift, kv_scale = _modulation(c_pad, kv_mod_w[None], kv_mod_b[None], 0, 2, batch)

    def mixer_pre_norms(l):
        own = (mix_pre_g[l], mix_mods[l][0], mix_mods[l][1])
        return [(kv_g, kv_shift, kv_scale), own] if l == n_a else [own]

    norms = _modnorm(h, mixer_pre_norms(0), seq)
    kv = None
    for l in range(depth):
        n = norms[-1]
        if l < n_a:
            z = _matmul(n, a_w_in[l].astype(BF16), BF16, epilogue="gelu", name="gmlp_in")
            gated = _gmlp_gate(z, a_ln_g[l], a_ln_b[l], a_w_s[l], a_b_s[l])
            out = _matmul(gated, a_w_out[l].astype(BF16), F32, name="gmlp_out")
        else:
            j = l - n_a
            if l == n_a:
                w_kv = jnp.concatenate([w_k, w_v], axis=1).astype(BF16)
                kv = _matmul(norms[0], w_kv, BF16, name="kv_proj").reshape(batch, seq, -1)
            q = _matmul(n, b_w_q[j].astype(BF16), BF16, epilogue="scale",
                        scale=LOG2E * dh ** -0.5, name="q_proj").reshape(batch, seq, -1)
            o = _diff_attention(q, kv, b_lq1[j], b_lk1[j], b_lq2[j], b_lk2[j], b_subln_g[j],
                                _lambda_init(l))
            out = _matmul(o.reshape(batch * seq, -1), b_w_o[j].astype(BF16), F32, name="attn_out")

        shift, scale, gate = ffn_mods[l]
        next_mods = mixer_pre_norms(l + 1) if l + 1 < depth else []
        h, norms = _moe_layer(h, out, mix_post_g[l], mix_mods[l][2], seq, ffn_pre_g[l], ffn_post_g[l],
                              shift, scale, gate, moe_w_r[l], moe_b_r[l], moe_w1p, l, moe_b1[l], moe_w2,
                              moe_b2[l], next_mods)
    return h.reshape(batch, seq, d)
```

```python
import functools
import math

import jax
import jax.numpy as jnp
from jax import lax
from jax.experimental import pallas as pl
from jax.experimental.pallas import tpu as pltpu

RMS_EPS = 1e-6
LN_EPS = 1e-5
TOP_K = 4
SWIGLU_ALPHA = 1.702
SWIGLU_LIMIT = 7.0
LOG2E = 1.4426950408889634

VMEM_LIMIT_BYTES = 56 * 1024 * 1024
LANES = 128
MOE_TILE = 256

BF16 = jnp.bfloat16
F32 = jnp.float32


def _params(*sem):
    return pltpu.CompilerParams(dimension_semantics=sem, vmem_limit_bytes=VMEM_LIMIT_BYTES)


def _rms(x):
    return x * lax.rsqrt(jnp.mean(x * x, axis=-1, keepdims=True) + RMS_EPS)


def _mod_kernel(c_ref, w_ref, b_ref, o_ref):
    c = c_ref[...]
    s = c * jax.nn.sigmoid(c)
    s_hi = s.astype(BF16)
    s_lo = (s - s_hi.astype(F32)).astype(BF16)
    lhs = jnp.concatenate([s_hi, s_lo], axis=0)
    r = jnp.dot(lhs, w_ref[...].astype(BF16), preferred_element_type=F32)
    rows = c.shape[0]
    o_ref[...] = r[:rows] + r[rows:] + b_ref[...]


def _modulation(c_pad, w3, b2, layer, n_split, batch):
    rows, d = c_pad.shape
    nout = w3.shape[-1]
    tn = min(512, nout)
    out = pl.pallas_call(
        _mod_kernel,
        grid=(nout // tn,),
        in_specs=[
            pl.BlockSpec((rows, d), lambda j: (0, 0)),
            pl.BlockSpec((None, d, tn), lambda j: (layer, 0, j)),
            pl.BlockSpec((None, 1, tn), lambda j: (layer, 0, j)),
        ],
        out_specs=pl.BlockSpec((rows, tn), lambda j: (0, j)),
        out_shape=jax.ShapeDtypeStruct((rows, nout), F32),
        compiler_params=_params("arbitrary"),
        name="modulation",
    )(c_pad, w3, b2.reshape(b2.shape[0], 1, nout))
    return [t[:batch, None, :] for t in jnp.split(out, n_split, axis=-1)]


def _modnorm_kernel(n_out, h_ref, *refs):
    y = _rms(h_ref[...])
    for t in range(n_out):
        g_ref, sh_ref, sc_ref = refs[3 * t:3 * t + 3]
        o_ref = refs[3 * n_out + t]
        o_ref[...] = ((y * g_ref[...]) * (1.0 + sc_ref[...]) + sh_ref[...]).astype(o_ref.dtype)


def _modnorm(h, mods, seq):
    n, d = h.shape
    tm = min(512, seq)
    per_b = seq // tm
    ins, specs = [h], [pl.BlockSpec((tm, d), lambda i: (i, 0))]
    for g, sh, sc in mods:
        ins += [g.reshape(1, d), sh, sc]
        specs += [pl.BlockSpec((1, d), lambda i: (0, 0)),
                  pl.BlockSpec((None, 1, d), lambda i: (i // per_b, 0, 0)),
                  pl.BlockSpec((None, 1, d), lambda i: (i // per_b, 0, 0))]
    outs = pl.pallas_call(
        functools.partial(_modnorm_kernel, len(mods)),
        grid=(n // tm,),
        in_specs=specs,
        out_specs=[pl.BlockSpec((tm, d), lambda i: (i, 0)) for _ in mods],
        out_shape=[jax.ShapeDtypeStruct((n, d), BF16) for _ in mods],
        compiler_params=_params("parallel"),
        name="modnorm",
    )(*ins)
    return list(outs)


def _matmul_kernel(epilogue, scale, x_ref, w_ref, o_ref):
    acc = jnp.dot(x_ref[...], w_ref[...], preferred_element_type=F32)
    if epilogue == "gelu":
        acc = 0.5 * acc * (1.0 + lax.erf(acc * (1.0 / math.sqrt(2.0))))
    elif epilogue == "scale":
        acc = acc * scale
    o_ref[...] = acc.astype(o_ref.dtype)


def _matmul(x, w, out_dtype, epilogue=None, scale=1.0, name="matmul"):
    m, k = x.shape
    nn = w.shape[1]
    tm = min(1024, m)
    tn = min(1024 if out_dtype == BF16 else 512, nn)
    return pl.pallas_call(
        functools.partial(_matmul_kernel, epilogue, scale),
        grid=(m // tm, nn // tn),
        in_specs=[pl.BlockSpec((tm, k), lambda i, j: (i, 0)),
                  pl.BlockSpec((k, tn), lambda i, j: (0, j))],
        out_specs=pl.BlockSpec((tm, tn), lambda i, j: (i, j)),
        out_shape=jax.ShapeDtypeStruct((m, nn), out_dtype),
        compiler_params=_params("parallel", "arbitrary"),
        name=name,
    )(x, w)


def _gate_kernel(chunk, groups, u_ref, v_ref, g_ref, b_ref, ws_ref, bs_ref, o_ref):
    v = v_ref[...].astype(F32)
    mu = jnp.mean(v, axis=-1, keepdims=True)
    vc = v - mu
    var = jnp.mean(vc * vc, axis=-1, keepdims=True)
    vln = (vc * lax.rsqrt(var + LN_EPS) * g_ref[...] + b_ref[...]).astype(BF16)
    tm, width = vln.shape
    gd = width // groups
    causal = (lax.broadcasted_iota(jnp.int32, (chunk, chunk), 0)
              >= lax.broadcasted_iota(jnp.int32, (chunk, chunk), 1))
    for g in range(groups):
        ws = jnp.where(causal, ws_ref[g], 0.0).astype(BF16)
        bias = bs_ref[:, g:g + 1]
        for c in range(tm // chunk):
            rows = slice(c * chunk, (c + 1) * chunk)
            cols = slice(g * gd, (g + 1) * gd)
            sv = jnp.dot(ws, vln[rows, cols], preferred_element_type=F32) + bias
            o_ref[rows, cols] = (u_ref[rows, cols].astype(F32) * sv).astype(o_ref.dtype)


def _gmlp_gate(z, ln_g, ln_b, w_s, b_s):
    n, w2 = z.shape
    width = w2 // 2
    groups, chunk, _ = w_s.shape
    tm = min(2 * chunk, n)
    return pl.pallas_call(
        functools.partial(_gate_kernel, chunk, groups),
        grid=(n // tm,),
        in_specs=[pl.BlockSpec((tm, width), lambda i: (i, 0)),
                  pl.BlockSpec((tm, width), lambda i: (i, 1)),
                  pl.BlockSpec((1, width), lambda i: (0, 0)),
                  pl.BlockSpec((1, width), lambda i: (0, 0)),
                  pl.BlockSpec((groups, chunk, chunk), lambda i: (0, 0, 0)),
                  pl.BlockSpec((chunk, groups), lambda i: (0, 0))],
        out_specs=pl.BlockSpec((tm, width), lambda i: (i, 0)),
        out_shape=jax.ShapeDtypeStruct((n, width), BF16),
        compiler_params=_params("parallel"),
        name="gmlp_gate",
    )(z, z, ln_g.reshape(1, width), ln_b.reshape(1, width), w_s, b_s.T)


ATTN_ROW_CHUNK = 256


def _attn_kernel(tq, tk, dh, lambda_init, q_ref, k_ref, v_ref, lq1_ref, lk1_ref, lq2_ref, lk2_ref,
                 sg_ref, o_ref, *scratch):
    s_ref = (scratch[0:2], scratch[2:4])
    p_ref = (scratch[4:6], scratch[6:8])
    al_ref = (scratch[8:10], scratch[10:12])
    m_ref, l_ref, acc_ref = scratch[12:14], scratch[14:16], scratch[16:18]
    qi = pl.program_id(2)
    dims = (((1,), (1,)), ((), ()))
    rc = min(ATTN_ROW_CHUNK, tq)

    def live_rows(diag):
        return slice(tk, tq) if diag == 1 else slice(0, tq)

    def scores(blk, buf, diag=None):
        start = pl.multiple_of(blk * tk, tk)
        rows = live_rows(diag)
        for mp in range(2):
            s_ref[buf][mp][rows, :] = lax.dot_general(
                q_ref[rows, mp * dh:(mp + 1) * dh], k_ref[pl.ds(start, tk), pl.ds(mp * dh, dh)], dims,
                preferred_element_type=F32)

    def accumulate(blk, buf, diag=None):
        start = pl.multiple_of(blk * tk, tk)
        rows = live_rows(diag)
        v = v_ref[pl.ds(start, tk), :]
        for mp in range(2):
            alpha = jnp.tile(al_ref[buf][mp][rows, :], (1, 2 * dh // LANES))
            acc_ref[mp][rows, :] = alpha * acc_ref[mp][rows, :] + jnp.dot(
                p_ref[buf][mp][rows, :], v, preferred_element_type=F32)

    def softmax(buf, diag=None):
        nblk = tk // LANES
        first = 0 if diag != 1 else tk // rc
        for mp in range(2):
            for c in range(first, tq // rc):
                rows = slice(c * rc, (c + 1) * rc)
                sb = []
                for j in range(nblk):
                    col_lo = (diag or 0) * tk + j * LANES
                    if diag is not None and col_lo > c * rc + rc - 1:
                        sb.append(None)
                        continue
                    s = s_ref[buf][mp][rows, j * LANES:(j + 1) * LANES]
                    if diag is not None and col_lo + LANES - 1 > c * rc:
                        keep = (lax.broadcasted_iota(jnp.int32, (rc, LANES), 1) + (col_lo - c * rc)
                                <= lax.broadcasted_iota(jnp.int32, (rc, LANES), 0))
                        s = jnp.where(keep, s, -jnp.inf)
                    sb.append(s)
                live = [s for s in sb if s is not None]
                part = live[0]
                for s in live[1:]:
                    part = jnp.maximum(part, s)
                m_old = m_ref[mp][rows, :]
                m_new = jnp.maximum(m_old, jnp.max(part, axis=-1, keepdims=True))
                pb = [None if s is None else jnp.exp2(s - m_new) for s in sb]
                live = [p for p in pb if p is not None]
                part = live[0]
                for p in live[1:]:
                    part = part + p
                alpha = jnp.exp2(m_old - m_new)
                l_ref[mp][rows, :] = alpha * l_ref[mp][rows, :] + jnp.sum(part, axis=-1, keepdims=True)
                m_ref[mp][rows, :] = m_new
                al_ref[buf][mp][rows, :] = alpha
                for j in range(nblk):
                    p_ref[buf][mp][rows, j * LANES:(j + 1) * LANES] = (
                        jnp.zeros((rc, LANES), BF16) if pb[j] is None else pb[j].astype(BF16))

    for mp in range(2):
        m_ref[mp][...] = jnp.full(m_ref[mp].shape, -jnp.inf, F32)
        l_ref[mp][...] = jnp.zeros(l_ref[mp].shape, F32)
        acc_ref[mp][...] = jnp.zeros(acc_ref[mp].shape, F32)
        p_ref[1][mp][...] = jnp.zeros(p_ref[1][mp].shape, BF16)
        al_ref[1][mp][...] = jnp.ones(al_ref[1][mp].shape, F32)
    scores(0, 0)

    def pair(t, _):
        scores(2 * t + 1, 1)
        accumulate(jnp.maximum(2 * t - 1, 0), 1)
        softmax(0)
        scores(2 * t + 2, 0)
        accumulate(2 * t, 0)
        softmax(1)
        return 0

    lax.fori_loop(0, qi, pair, 0)
    scores(2 * qi + 1, 1, diag=1)
    accumulate(jnp.maximum(2 * qi - 1, 0), 1)
    softmax(0, diag=0)
    accumulate(2 * qi, 0)
    softmax(1, diag=1)
    accumulate(2 * qi + 1, 1, diag=1)

    lam = (jnp.exp(jnp.sum(lq1_ref[...] * lk1_ref[...], axis=-1, keepdims=True))
           - jnp.exp(jnp.sum(lq2_ref[...] * lk2_ref[...], axis=-1, keepdims=True)) + lambda_init)
    inv1 = 1.0 / l_ref[0][...]
    inv2 = lam / l_ref[1][...]
    o = jnp.concatenate(
        [acc_ref[0][:, j * LANES:(j + 1) * LANES] * inv1 - acc_ref[1][:, j * LANES:(j + 1) * LANES] * inv2
         for j in range(2 * dh // LANES)], axis=1)
    o = _rms(o) * sg_ref[...] * (1.0 - lambda_init)
    o_ref[...] = o.astype(o_ref.dtype)


def _diff_attention(q, kv, lq1, lk1, lq2, lk2, subln_g, lambda_init):
    b, s, aw = q.shape
    dh = lq1.shape[-1]
    heads = aw // (2 * dh)
    tq = min(1024, s)
    tk = tq // 2
    vec = lambda a: a.reshape(1, -1)
    small = lambda w: pl.BlockSpec((1, w), lambda bi, hi, qi: (0, 0))
    return pl.pallas_call(
        functools.partial(_attn_kernel, tq, tk, dh, lambda_init),
        grid=(b, heads, s // tq),
        in_specs=[pl.BlockSpec((None, tq, 2 * dh), lambda bi, hi, qi: (bi, qi, hi)),
                  pl.BlockSpec((None, s, 2 * dh), lambda bi, hi, qi: (bi, 0, hi)),
                  pl.BlockSpec((None, s, 2 * dh), lambda bi, hi, qi: (bi, 0, heads + hi)),
                  small(dh), small(dh), small(dh), small(dh), small(2 * dh)],
        out_specs=pl.BlockSpec((None, tq, 2 * dh), lambda bi, hi, qi: (bi, qi, hi)),
        out_shape=jax.ShapeDtypeStruct((b, s, aw), BF16),
        scratch_shapes=([pltpu.VMEM((tq, tk), F32)] * 4 + [pltpu.VMEM((tq, tk), BF16)] * 4
                        + [pltpu.VMEM((tq, LANES), F32)] * 8 + [pltpu.VMEM((tq, 2 * dh), F32)] * 2),
        compiler_params=_params("parallel", "parallel", "arbitrary"),
        name="diff_attention",
    )(q, kv, kv, vec(lq1), vec(lk1), vec(lq2), vec(lk2), vec(subln_g))


def _router_kernel(h_ref, y_ref, pg_ref, pgate_ref, g_ref, sh_ref, sc_ref, wr_ref, br_ref,
                   h1_ref, np_ref, idx_ref, gate_ref, rank_ref, cnt_ref, carry_ref):
    i = pl.program_id(0)

    @pl.when(i == 0)
    def _():
        carry_ref[...] = jnp.zeros_like(carry_ref)

    h1 = h_ref[...] + pgate_ref[...] * (_rms(y_ref[...]) * pg_ref[...])
    h1_ref[...] = h1
    n = (_rms(h1) * g_ref[...]) * (1.0 + sc_ref[...]) + sh_ref[...]
    tm, d = n.shape
    n_hi = n.astype(BF16)
    n_hi32 = n_hi.astype(F32)
    n_lo = (n - n_hi32).astype(BF16)

    bits = pltpu.bitcast(n_hi32, jnp.uint32)
    half = d // 2
    np_ref[...] = (lax.shift_right_logical(bits[:, :half], jnp.uint32(16))
                   | (bits[:, half:] & jnp.uint32(0xFFFF0000)))

    wr = wr_ref[...]
    w_hi = wr.astype(BF16)
    w_lo = (wr - w_hi.astype(F32)).astype(BF16)
    dims = (((1,), (1,)), ((), ()))
    logits = (lax.dot_general(w_hi, n_hi, dims, preferred_element_type=F32)
              + lax.dot_general(w_lo, n_hi, dims, preferred_element_type=F32)
              + lax.dot_general(w_hi, n_lo, dims, preferred_element_type=F32)
              + br_ref[...])
    n_exp = logits.shape[0]
    eid = lax.broadcasted_iota(jnp.int32, (n_exp, tm), 0).astype(F32)

    vals, sels, ids = [], [], []
    rem = logits
    for _ in range(TOP_K):
        m = jnp.max(rem, axis=0, keepdims=True)
        first = jnp.min(jnp.where(rem == m, eid, float(n_exp)), axis=0, keepdims=True)
        sel = eid == first
        rem = jnp.where(sel, -jnp.inf, rem)
        vals.append(m)
        sels.append(sel)
        ids.append(first)

    exps = [jnp.exp(v - vals[0]) for v in vals]
    denom = exps[0]
    for e in exps[1:]:
        denom = denom + e

    chosen = sels[0]
    for sel in sels[1:]:
        chosen = jnp.logical_or(chosen, sel)
    chosen = jnp.where(chosen, 1.0, 0.0)
    before = (lax.broadcasted_iota(jnp.int32, (tm, tm), 0)
              < lax.broadcasted_iota(jnp.int32, (tm, tm), 1))
    upper = jnp.where(before, 1.0, 0.0).astype(BF16)
    rank_excl = jnp.dot(chosen.astype(BF16), upper, preferred_element_type=F32) + carry_ref[...]

    for k in range(TOP_K):
        idx_ref[k:k + 1, :] = ids[k].astype(jnp.int32)
        gate_ref[k:k + 1, :] = exps[k] / denom
        rank_ref[k:k + 1, :] = jnp.sum(jnp.where(sels[k], rank_excl, 0.0), axis=0,
                                       keepdims=True).astype(jnp.int32)

    total = carry_ref[...] + jnp.sum(chosen, axis=1, keepdims=True)
    carry_ref[...] = total
    cnt_ref[...] = jnp.broadcast_to(total, cnt_ref.shape).astype(jnp.int32)


def _router(h, y, post_g, post_gate, g, shift, scale, w_r, b_r, seq):
    n, d = h.shape
    n_exp = w_r.shape[1]
    tm = min(256, seq)
    per_b = seq // tm
    tok = lambda dt: jax.ShapeDtypeStruct((TOP_K, n), dt)
    tok_spec = pl.BlockSpec((TOP_K, tm), lambda i: (0, i))
    row_spec = pl.BlockSpec((tm, d), lambda i: (i, 0))
    vec_spec = pl.BlockSpec((1, d), lambda i: (0, 0))
    mod_spec = pl.BlockSpec((None, 1, d), lambda i: (i // per_b, 0, 0))
    return pl.pallas_call(
        _router_kernel,
        grid=(n // tm,),
        in_specs=[row_spec, row_spec, vec_spec, mod_spec, vec_spec, mod_spec, mod_spec,
                  pl.BlockSpec((n_exp, d), lambda i: (0, 0)),
                  pl.BlockSpec((n_exp, 1), lambda i: (0, 0))],
        out_specs=[row_spec, pl.BlockSpec((tm, d // 2), lambda i: (i, 0)), tok_spec, tok_spec, tok_spec,
                   pl.BlockSpec((n_exp, LANES), lambda i: (0, 0))],
        out_shape=[jax.ShapeDtypeStruct((n, d), F32), jax.ShapeDtypeStruct((n, d // 2), jnp.uint32),
                   tok(jnp.int32), tok(F32), tok(jnp.int32),
                   jax.ShapeDtypeStruct((n_exp, LANES), jnp.int32)],
        scratch_shapes=[pltpu.VMEM((n_exp, 1), F32)],
        compiler_params=_params("arbitrary"),
        name="moe_router",
    )(h, y, post_g.reshape(1, d), post_gate, g.reshape(1, d), shift, scale, w_r.T,
      b_r.reshape(n_exp, 1))


def _dispatch_kernel(n_tok, tm, dest_ref, ztile_ref, np_ref, xs_ref, zeros, sem, zsem):
    i = pl.program_id(0)

    @pl.when(i == 0)
    def _():
        zeros[...] = jnp.zeros_like(zeros)
        rows = zeros.shape[0]

        def fill(t):
            return pltpu.make_async_copy(zeros, xs_ref.at[pl.ds(t * rows, rows)], zsem)

        def start(t, _):
            @pl.when(ztile_ref[t] == 1)
            def _():
                fill(t).start()
            return 0

        def finish(t, _):
            @pl.when(ztile_ref[t] == 1)
            def _():
                fill(t).wait()
            return 0

        n_tiles = xs_ref.shape[0] // rows
        lax.fori_loop(0, n_tiles, start, 0)
        lax.fori_loop(0, n_tiles, finish, 0)

    def row_copy(r, k):
        d = dest_ref[k * n_tok + i * tm + r]
        return pltpu.make_async_copy(np_ref.at[pl.ds(r, 1)], xs_ref.at[pl.ds(d, 1)], sem)

    def issue(r, _):
        for k in range(TOP_K):
            row_copy(r, k).start()
        return 0

    lax.fori_loop(0, tm, issue, 0)

    def drain(r, _):
        for k in range(TOP_K):
            row_copy(r, k).wait()
        return 0

    lax.fori_loop(0, tm, drain, 0)


def _dispatch(n_packed, dest_flat, zero_tile):
    n, half = n_packed.shape
    tm = min(256, n)
    n_rows = zero_tile.shape[0] * MOE_TILE
    return pl.pallas_call(
        functools.partial(_dispatch_kernel, n, tm),
        grid_spec=pltpu.PrefetchScalarGridSpec(
            num_scalar_prefetch=2,
            grid=(n // tm,),
            in_specs=[pl.BlockSpec((tm, half), lambda i, dest, zt: (i, 0))],
            out_specs=pl.BlockSpec(memory_space=pl.ANY),
            scratch_shapes=[pltpu.VMEM((MOE_TILE, half), jnp.uint32),
                            pltpu.SemaphoreType.DMA(()), pltpu.SemaphoreType.DMA(())]),
        out_shape=jax.ShapeDtypeStruct((n_rows, half), jnp.uint32),
        compiler_params=_params("arbitrary"),
        name="moe_dispatch",
    )(dest_flat, zero_tile, n_packed)


def _w1_prep_kernel(group, w_ref, p_ref, o_ref):
    for cb in range(w_ref.shape[1] // group):
        cols = slice(cb * group, (cb + 1) * group)
        o_ref[:, cols] = jnp.dot(w_ref[:, cols].astype(BF16), p_ref[...],
                                 preferred_element_type=F32).astype(BF16)


def _w1_prep(w1):
    shape = w1.shape
    cols = shape[-1]
    rows = math.prod(shape[:-1])
    group = min(2 * LANES, cols)
    tm = min(2048, rows)
    src = lax.broadcasted_iota(jnp.int32, (group, group), 0)
    dst = lax.broadcasted_iota(jnp.int32, (group, group), 1)
    perm = (dst == src // 2 + (group // 2) * (src % 2)).astype(BF16)
    out = pl.pallas_call(
        functools.partial(_w1_prep_kernel, group),
        grid=(rows // tm,),
        in_specs=[pl.BlockSpec((tm, cols), lambda i: (i, 0)),
                  pl.BlockSpec((group, group), lambda i: (0, 0))],
        out_specs=pl.BlockSpec((tm, cols), lambda i: (i, 0)),
        out_shape=jax.ShapeDtypeStruct((rows, cols), BF16),
        compiler_params=_params("parallel"),
        name="moe_w1_prep",
    )(w1.reshape(rows, cols), perm)
    return out.reshape(shape)


def _deinterleave_blocks(h, group):
    half = group // 2
    blocks = range(h.shape[1] // group)
    glu = jnp.concatenate([h[:, b * group:b * group + half] for b in blocks], axis=1)
    lin = jnp.concatenate([h[:, b * group + half:(b + 1) * group] for b in blocks], axis=1)
    return glu, lin


def _expert_kernel(group, texp_ref, nused_ref, first_ref, x_ref, w1_ref, w2_ref, b1g_ref, b1l_ref, b2_ref,
                   o_ref, w2b_ref):
    i = pl.program_id(0)

    @pl.when(first_ref[i] == 1)
    def _():
        w2b_ref[...] = w2_ref[...].astype(BF16)

    @pl.when(i < nused_ref[0])
    def _():
        w = x_ref[...]
        half = w.shape[1]
        x_lo = pltpu.bitcast(lax.shift_left(w, jnp.uint32(16)), F32).astype(BF16)
        x_hi = pltpu.bitcast(w & jnp.uint32(0xFFFF0000), F32).astype(BF16)
        hcat = (jnp.dot(x_lo, w1_ref[:half, :], preferred_element_type=F32)
                + jnp.dot(x_hi, w1_ref[half:, :], preferred_element_type=F32))
        glu, lin = _deinterleave_blocks(hcat, group)
        glu = jnp.minimum(glu + b1g_ref[...], SWIGLU_LIMIT)
        lin = jnp.clip(lin + b1l_ref[...], -SWIGLU_LIMIT, SWIGLU_LIMIT)
        act = glu * jax.nn.sigmoid(SWIGLU_ALPHA * glu) * (lin + 1.0)
        o_ref[...] = jnp.dot(act.astype(BF16), w2b_ref[...], preferred_element_type=F32) + b2_ref[...]

    @pl.when(i >= nused_ref[0])
    def _():
        o_ref[...] = jnp.zeros_like(o_ref)


def _experts(xs, tile_exp, n_used, first_tile, w1p, w2, layer, b1g, b1l, b2):
    n_rows, half = xs.shape
    _, n_exp, d, ff2 = w1p.shape
    ff = ff2 // 2
    n_tiles = n_rows // MOE_TILE
    tile = lambda i, te, nu, ft: (jnp.minimum(i, nu[0] - 1), 0)
    wmap = lambda i, te, nu, ft: (te[jnp.minimum(i, nu[0] - 1)], 0, 0)
    lwmap = lambda i, te, nu, ft: (layer, te[jnp.minimum(i, nu[0] - 1)], 0, 0)
    return pl.pallas_call(
        functools.partial(_expert_kernel, min(2 * LANES, ff2)),
        grid_spec=pltpu.PrefetchScalarGridSpec(
            num_scalar_prefetch=3,
            grid=(n_tiles,),
            in_specs=[pl.BlockSpec((MOE_TILE, half), tile),
                      pl.BlockSpec((None, None, d, ff2), lwmap),
                      pl.BlockSpec((None, None, ff, d), lwmap),
                      pl.BlockSpec((None, 1, ff), wmap),
                      pl.BlockSpec((None, 1, ff), wmap),
                      pl.BlockSpec((None, 1, d), wmap)],
            out_specs=pl.BlockSpec((MOE_TILE, d), lambda i, te, nu, ft: (i, 0)),
            scratch_shapes=[pltpu.VMEM((ff, d), BF16)]),
        out_shape=jax.ShapeDtypeStruct((n_rows, d), F32),
        compiler_params=_params("arbitrary"),
        name="moe_experts",
    )(tile_exp, n_used, first_tile, xs, w1p, w2, b1g, b1l, b2)


COMBINE_SLOTS = 3
COMBINE_ROW_CHUNK = 16


def _combine_kernel(n_tok, tm, n_mods, dest_ref, yb_ref, gt_ref, h_ref, g_ref, gate_ref, *refs):
    mod_refs = refs[:3 * n_mods]
    o_ref = refs[3 * n_mods]
    n_refs = refs[3 * n_mods + 1:4 * n_mods + 1]
    buf, sems = refs[4 * n_mods + 1:]
    i = pl.program_id(0)
    steps = pl.num_programs(0)

    def row_copy(step, slot, r, k):
        d = dest_ref[k * n_tok + step * tm + r]
        return pltpu.make_async_copy(yb_ref.at[pl.ds(d, 1)], buf.at[slot, k, pl.ds(r, 1)], sems.at[slot])

    def issue_tile(step, slot):
        def body(r, _):
            for k in range(TOP_K):
                row_copy(step, slot, r, k).start()
            return 0
        lax.fori_loop(0, tm, body, 0)

    def wait_tile(step, slot):
        def body(r, _):
            for k in range(TOP_K):
                row_copy(step, slot, r, k).wait()
            return 0
        lax.fori_loop(0, tm, body, 0)

    last = steps - 1

    @pl.when(i == 0)
    def _():
        issue_tile(0, 0)
        issue_tile(jnp.minimum(1, last), 1)

    slot = i % COMBINE_SLOTS
    wait_tile(i, slot)
    ahead = jnp.minimum(i + 2, last)
    ahead_slot = (i + 2) % COMBINE_SLOTS

    def rows_step(c, _):
        r0 = pl.multiple_of(c * COMBINE_ROW_CHUNK, COMBINE_ROW_CHUNK)
        rows = pl.ds(r0, COMBINE_ROW_CHUNK)
        y = buf[slot, 0, rows, :] * gt_ref[rows, 0:1]
        for k in range(1, TOP_K):
            y = y + buf[slot, k, rows, :] * gt_ref[rows, k:k + 1]
        h_new = h_ref[rows, :] + gate_ref[...] * (_rms(y) * g_ref[...])
        o_ref[rows, :] = h_new
        if n_mods:
            yn = _rms(h_new)
            for t in range(n_mods):
                mg_ref, sh_ref, sc_ref = mod_refs[3 * t:3 * t + 3]
                n_refs[t][rows, :] = ((yn * mg_ref[...]) * (1.0 + sc_ref[...]) + sh_ref[...]).astype(BF16)
        for rr in range(COMBINE_ROW_CHUNK):
            for k in range(TOP_K):
                row_copy(ahead, ahead_slot, r0 + rr, k).start()
        return 0

    lax.fori_loop(0, tm // COMBINE_ROW_CHUNK, rows_step, 0)

    @pl.when(i == last)
    def _():
        wait_tile(last, (i + 1) % COMBINE_SLOTS)
        wait_tile(last, (i + 2) % COMBINE_SLOTS)


def _combine(yb, dest_flat, gates_t, h, g, gate, seq, next_mods):
    n, d = h.shape
    tm = min(128, seq)
    per_b = seq // tm
    row_spec = pl.BlockSpec((tm, d), lambda i, dest: (i, 0))
    vec_spec = pl.BlockSpec((1, d), lambda i, dest: (0, 0))
    mod_spec = pl.BlockSpec((None, 1, d), lambda i, dest: (i // per_b, 0, 0))
    mod_ins, mod_specs = [], []
    for mg, sh, sc in next_mods:
        mod_ins += [mg.reshape(1, d), sh, sc]
        mod_specs += [vec_spec, mod_spec, mod_spec]
    outs = pl.pallas_call(
        functools.partial(_combine_kernel, n, tm, len(next_mods)),
        grid_spec=pltpu.PrefetchScalarGridSpec(
            num_scalar_prefetch=1,
            grid=(n // tm,),
            in_specs=[pl.BlockSpec(memory_space=pl.ANY),
                      pl.BlockSpec((tm, TOP_K), lambda i, dest: (i, 0)),
                      row_spec, vec_spec, mod_spec] + mod_specs,
            out_specs=[row_spec] * (1 + len(next_mods)),
            scratch_shapes=[pltpu.VMEM((COMBINE_SLOTS, TOP_K, tm, d), F32),
                            pltpu.SemaphoreType.DMA((COMBINE_SLOTS,))]),
        out_shape=[jax.ShapeDtypeStruct((n, d), F32)]
        + [jax.ShapeDtypeStruct((n, d), BF16) for _ in next_mods],
        compiler_params=_params("arbitrary"),
        name="moe_combine",
    )(dest_flat, yb, gates_t, h, g.reshape(1, d), gate, *mod_ins)
    return outs[0], list(outs[1:])


def _moe_layer(h, mix_out, mix_post_g, mix_gate, seq, pre_g, post_g, shift, scale, gate, w_r, b_r, w1p,
               layer, b1, w2, b2, next_mods):
    n, d = h.shape
    n_exp = w_r.shape[1]
    h, n_packed, idx, gates, rank, cnt = _router(h, mix_out, mix_post_g, mix_gate, pre_g, shift, scale,
                                                 w_r, b_r, seq)

    counts = cnt[:, 0]
    tiles_e = (counts + MOE_TILE - 1) // MOE_TILE
    tile_end = jnp.cumsum(tiles_e)
    row_start = (tile_end - tiles_e) * MOE_TILE
    experts = jnp.arange(n_exp, dtype=jnp.int32)[:, None, None]
    dest = jnp.sum(jnp.where(idx[None] == experts, row_start[:, None, None], 0), axis=0) + rank
    dest = dest.reshape(-1).astype(jnp.int32)
    n_tiles = (n * TOP_K) // MOE_TILE + n_exp
    tile_exp = jnp.sum(tile_end[None, :] <= jnp.arange(n_tiles)[:, None], axis=1)
    tile_exp = jnp.minimum(tile_exp, n_exp - 1).astype(jnp.int32)
    n_used = tile_end[-1:].astype(jnp.int32)
    tile_ids = jnp.arange(n_tiles)
    last_of_expert = jnp.any((tile_ids[:, None] == tile_end[None, :] - 1) & (tiles_e[None, :] > 0), axis=1)
    zero_tile = (last_of_expert | (tile_ids >= n_used[0])).astype(jnp.int32)
    first_tile = jnp.any((tile_ids[:, None] == (tile_end - tiles_e)[None, :]) & (tiles_e[None, :] > 0),
                         axis=1).astype(jnp.int32)

    xs = _dispatch(n_packed, dest, zero_tile)
    ff = w2.shape[2]
    yb = _experts(xs, tile_exp, n_used, first_tile, w1p, w2, layer,
                  b1[:, 0::2].reshape(n_exp, 1, ff), b1[:, 1::2].reshape(n_exp, 1, ff),
                  b2.reshape(n_exp, 1, d))
    return _combine(yb, dest, gates.T, h, post_g, gate, seq, next_mods)


def _lambda_init(layer):
    return 0.8 - 0.6 * math.exp(-0.3 * layer)


def kernel(x, c, mix_pre_g, mix_post_g, mix_mod_w, mix_mod_b, ffn_pre_g, ffn_post_g, ffn_mod_w, ffn_mod_b, a_w_in, a_ln_g, a_ln_b, a_w_s, a_b_s, a_w_out, kv_g, kv_mod_w, kv_mod_b, w_k, w_v, b_w_q, b_lq1, b_lk1, b_lq2, b_lk2, b_subln_g, b_w_o, moe_w_r, moe_b_r, moe_w1, moe_b1, moe_w2, moe_b2):
    batch, seq, d = x.shape
    depth = mix_pre_g.shape[0]
    n_a = a_w_in.shape[0]
    dh = b_lq1.shape[-1]
    h = x.reshape(batch * seq, d)
    c_pad = jnp.zeros((16, d), F32).at[:batch].set(c)
    moe_w1p = _w1_prep(moe_w1)

    mix_mods = [_modulation(c_pad, mix_mod_w, mix_mod_b, l, 3, batch) for l in range(depth)]
    ffn_mods = [_modulation(c_pad, ffn_mod_w, ffn_mod_b, l, 3, batch) for l in range(depth)]
    if depth > n_a:
        kv_shift, kv_scale = _modulation(c_pad, kv_mod_w[None], kv_mod_b[None], 0, 2, batch)

    def mixer_pre_norms(l):
        own = (mix_pre_g[l], mix_mods[l][0], mix_mods[l][1])
        return [(kv_g, kv_shift, kv_scale), own] if l == n_a else [own]

    norms = _modnorm(h, mixer_pre_norms(0), seq)
    kv = None
    for l in range(depth):
        n = norms[-1]
        if l < n_a:
            z = _matmul(n, a_w_in[l].astype(BF16), BF16, epilogue="gelu", name="gmlp_in")
            gated = _gmlp_gate(z, a_ln_g[l], a_ln_b[l], a_w_s[l], a_b_s[l])
            out = _matmul(gated, a_w_out[l].astype(BF16), F32, name="gmlp_out")
        else:
            j = l - n_a
            if l == n_a:
                w_kv = jnp.concatenate([w_k, w_v], axis=1).astype(BF16)
                kv = _matmul(norms[0], w_kv, BF16, name="kv_proj").reshape(batch, seq, -1)
            q = _matmul(n, b_w_q[j].astype(BF16), BF16, epilogue="scale",
                        scale=LOG2E * dh ** -0.5, name="q_proj").reshape(batch, seq, -1)
            o = _diff_attention(q, kv, b_lq1[j], b_lk1[j], b_lq2[j], b_lk2[j], b_subln_g[j],
                                _lambda_init(l))
            out = _matmul(o.reshape(batch * seq, -1), b_w_o[j].astype(BF16), F32, name="attn_out")

        shift, scale, gate = ffn_mods[l]
        next_mods = mixer_pre_norms(l + 1) if l + 1 < depth else []
        h, norms = _moe_layer(h, out, mix_post_g[l], mix_mods[l][2], seq, ffn_pre_g[l], ffn_post_g[l],
                              shift, scale, gate, moe_w_r[l], moe_b_r[l], moe_w1p, l, moe_b1[l], moe_w2,
                              moe_b2[l], next_mods)
    return h.reshape(batch, seq, d)
```

```python
import functools
import math

import jax
import jax.numpy as jnp
from jax import lax
from jax.experimental import pallas as pl
from jax.experimental.pallas import tpu as pltpu

RMS_EPS = 1e-6
LN_EPS = 1e-5
TOP_K = 4
SWIGLU_ALPHA = 1.702
SWIGLU_LIMIT = 7.0
LOG2E = 1.4426950408889634

VMEM_LIMIT_BYTES = 56 * 1024 * 1024
LANES = 128
MOE_TILE = 256

BF16 = jnp.bfloat16
F32 = jnp.float32


def _params(*sem):
    return pltpu.CompilerParams(dimension_semantics=sem, vmem_limit_bytes=VMEM_LIMIT_BYTES)


def _rms(x):
    return x * lax.rsqrt(jnp.mean(x * x, axis=-1, keepdims=True) + RMS_EPS)


def _mod_kernel(c_ref, w_ref, b_ref, o_ref):
    c = c_ref[...]
    s = c * jax.nn.sigmoid(c)
    s_hi = s.astype(BF16)
    s_lo = (s - s_hi.astype(F32)).astype(BF16)
    lhs = jnp.concatenate([s_hi, s_lo], axis=0)
    r = jnp.dot(lhs, w_ref[...].astype(BF16), preferred_element_type=F32)
    rows = c.shape[0]
    o_ref[...] = r[:rows] + r[rows:] + b_ref[...]


def _modulation(c_pad, w3, b2, layer, n_split, batch):
    rows, d = c_pad.shape
    nout = w3.shape[-1]
    tn = min(512, nout)
    out = pl.pallas_call(
        _mod_kernel,
        grid=(nout // tn,),
        in_specs=[
            pl.BlockSpec((rows, d), lambda j: (0, 0)),
            pl.BlockSpec((None, d, tn), lambda j: (layer, 0, j)),
            pl.BlockSpec((None, 1, tn), lambda j: (layer, 0, j)),
        ],
        out_specs=pl.BlockSpec((rows, tn), lambda j: (0, j)),
        out_shape=jax.ShapeDtypeStruct((rows, nout), F32),
        compiler_params=_params("arbitrary"),
        name="modulation",
    )(c_pad, w3, b2.reshape(b2.shape[0], 1, nout))
    return [t[:batch, None, :] for t in jnp.split(out, n_split, axis=-1)]


def _modnorm_kernel(n_out, h_ref, *refs):
    y = _rms(h_ref[...])
    for t in range(n_out):
        g_ref, sh_ref, sc_ref = refs[3 * t:3 * t + 3]
        o_ref = refs[3 * n_out + t]
        o_ref[...] = ((y * g_ref[...]) * (1.0 + sc_ref[...]) + sh_ref[...]).astype(o_ref.dtype)


def _modnorm(h, mods, seq):
    n, d = h.shape
    tm = min(512, seq)
    per_b = seq // tm
    ins, specs = [h], [pl.BlockSpec((tm, d), lambda i: (i, 0))]
    for g, sh, sc in mods:
        ins += [g.reshape(1, d), sh, sc]
        specs += [pl.BlockSpec((1, d), lambda i: (0, 0)),
                  pl.BlockSpec((None, 1, d), lambda i: (i // per_b, 0, 0)),
                  pl.BlockSpec((None, 1, d), lambda i: (i // per_b, 0, 0))]
    outs = pl.pallas_call(
        functools.partial(_modnorm_kernel, len(mods)),
        grid=(n // tm,),
        in_specs=specs,
        out_specs=[pl.BlockSpec((tm, d), lambda i: (i, 0)) for _ in mods],
        out_shape=[jax.ShapeDtypeStruct((n, d), BF16) for _ in mods],
        compiler_params=_params("parallel"),
        name="modnorm",
    )(*ins)
    return list(outs)


def _matmul_kernel(epilogue, scale, x_ref, w_ref, o_ref):
    acc = jnp.dot(x_ref[...], w_ref[...], preferred_element_type=F32)
    if epilogue == "gelu":
        acc = 0.5 * acc * (1.0 + lax.erf(acc * (1.0 / math.sqrt(2.0))))
    elif epilogue == "scale":
        acc = acc * scale
    o_ref[...] = acc.astype(o_ref.dtype)


def _matmul(x, w, out_dtype, epilogue=None, scale=1.0, name="matmul"):
    m, k = x.shape
    nn = w.shape[1]
    tm = min(1024, m)
    tn = min(1024 if out_dtype == BF16 else 512, nn)
    return pl.pallas_call(
        functools.partial(_matmul_kernel, epilogue, scale),
        grid=(m // tm, nn // tn),
        in_specs=[pl.BlockSpec((tm, k), lambda i, j: (i, 0)),
                  pl.BlockSpec((k, tn), lambda i, j: (0, j))],
        out_specs=pl.BlockSpec((tm, tn), lambda i, j: (i, j)),
        out_shape=jax.ShapeDtypeStruct((m, nn), out_dtype),
        compiler_params=_params("parallel", "arbitrary"),
        name=name,
    )(x, w)


def _gate_kernel(chunk, groups, u_ref, v_ref, g_ref, b_ref, ws_ref, bs_ref, o_ref):
    v = v_ref[...].astype(F32)
    mu = jnp.mean(v, axis=-1, keepdims=True)
    vc = v - mu
    var = jnp.mean(vc * vc, axis=-1, keepdims=True)
    vln = (vc * lax.rsqrt(var + LN_EPS) * g_ref[...] + b_ref[...]).astype(BF16)
    tm, width = vln.shape
    gd = width // groups
    causal = (lax.broadcasted_iota(jnp.int32, (chunk, chunk), 0)
              >= lax.broadcasted_iota(jnp.int32, (chunk, chunk), 1))
    for g in range(groups):
        ws = jnp.where(causal, ws_ref[g], 0.0).astype(BF16)
        bias = bs_ref[:, g:g + 1]
        for c in range(tm // chunk):
            rows = slice(c * chunk, (c + 1) * chunk)
            cols = slice(g * gd, (g + 1) * gd)
            sv = jnp.dot(ws, vln[rows, cols], preferred_element_type=F32) + bias
            o_ref[rows, cols] = (u_ref[rows, cols].astype(F32) * sv).astype(o_ref.dtype)


def _gmlp_gate(z, ln_g, ln_b, w_s, b_s):
    n, w2 = z.shape
    width = w2 // 2
    groups, chunk, _ = w_s.shape
    tm = min(2 * chunk, n)
    return pl.pallas_call(
        functools.partial(_gate_kernel, chunk, groups),
        grid=(n // tm,),
        in_specs=[pl.BlockSpec((tm, width), lambda i: (i, 0)),
                  pl.BlockSpec((tm, width), lambda i: (i, 1)),
                  pl.BlockSpec((1, width), lambda i: (0, 0)),
                  pl.BlockSpec((1, width), lambda i: (0, 0)),
                  pl.BlockSpec((groups, chunk, chunk), lambda i: (0, 0, 0)),
                  pl.BlockSpec((chunk, groups), lambda i: (0, 0))],
        out_specs=pl.BlockSpec((tm, width), lambda i: (i, 0)),
        out_shape=jax.ShapeDtypeStruct((n, width), BF16),
        compiler_params=_params("parallel"),
        name="gmlp_gate",
    )(z, z, ln_g.reshape(1, width), ln_b.reshape(1, width), w_s, b_s.T)


ATTN_ROW_CHUNK = 256


def _attn_kernel(tq, tk, dh, lambda_init, q_ref, k_ref, v_ref, lq1_ref, lk1_ref, lq2_ref, lk2_ref,
                 sg_ref, o_ref, *scratch):
    s_ref = (scratch[0:2], scratch[2:4])
    p_ref = (scratch[4:6], scratch[6:8])
    al_ref = (scratch[8:10], scratch[10:12])
    m_ref, l_ref, acc_ref = scratch[12:14], scratch[14:16], scratch[16:18]
    qi = pl.program_id(2)
    dims = (((1,), (1,)), ((), ()))
    rc = min(ATTN_ROW_CHUNK, tq)

    def live_rows(diag):
        return slice(tk, tq) if diag == 1 else slice(0, tq)

    def scores(blk, buf, diag=None):
        start = pl.multiple_of(blk * tk, tk)
        rows = live_rows(diag)
        for mp in range(2):
            s_ref[buf][mp][rows, :] = lax.dot_general(
                q_ref[rows, mp * dh:(mp + 1) * dh], k_ref[pl.ds(start, tk), pl.ds(mp * dh, dh)], dims,
                preferred_element_type=F32)

    def accumulate(blk, buf, diag=None):
        start = pl.multiple_of(blk * tk, tk)
        rows = live_rows(diag)
        v = v_ref[pl.ds(start, tk), :]
        for mp in range(2):
            alpha = jnp.tile(al_ref[buf][mp][rows, :], (1, 2 * dh // LANES))
            acc_ref[mp][rows, :] = alpha * acc_ref[mp][rows, :] + jnp.dot(
                p_ref[buf][mp][rows, :], v, preferred_element_type=F32)

    def softmax(buf, diag=None):
        nblk = tk // LANES
        first = 0 if diag != 1 else tk // rc
        for mp in range(2):
            for c in range(first, tq // rc):
                rows = slice(c * rc, (c + 1) * rc)
                sb = []
                for j in range(nblk):
                    col_lo = (diag or 0) * tk + j * LANES
                    if diag is not None and col_lo > c * rc + rc - 1:
                        sb.append(None)
                        continue
                    s = s_ref[buf][mp][rows, j * LANES:(j + 1) * LANES]
                    if diag is not None and col_lo + LANES - 1 > c * rc:
                        keep = (lax.broadcasted_iota(jnp.int32, (rc, LANES), 1) + (col_lo - c * rc)
                                <= lax.broadcasted_iota(jnp.int32, (rc, LANES), 0))
                        s = jnp.where(keep, s, -jnp.inf)
                    sb.append(s)
                live = [s for s in sb if s is not None]
                part = live[0]
                for s in live[1:]:
                    part = jnp.maximum(part, s)
                m_old = m_ref[mp][rows, :]
                m_new = jnp.maximum(m_old, jnp.max(part, axis=-1, keepdims=True))
                pb = [None if s is None else jnp.exp2(s - m_new) for s in sb]
                live = [p for p in pb if p is not None]
                part = live[0]
                for p in live[1:]:
                    part = part + p
                alpha = jnp.exp2(m_old - m_new)
                l_ref[mp][rows, :] = alpha * l_ref[mp][rows, :] + jnp.sum(part, axis=-1, keepdims=True)
                m_ref[mp][rows, :] = m_new
                al_ref[buf][mp][rows, :] = alpha
                for j in range(nblk):
                    p_ref[buf][mp][rows, j * LANES:(j + 1) * LANES] = (
                        jnp.zeros((rc, LANES), BF16) if pb[j] is None else pb[j].astype(BF16))

    for mp in range(2):
        m_ref[mp][...] = jnp.full(m_ref[mp].shape, -jnp.inf, F32)
        l_ref[mp][...] = jnp.zeros(l_ref[mp].shape, F32)
        acc_ref[mp][...] = jnp.zeros(acc_ref[mp].shape, F32)
        p_ref[1][mp][...] = jnp.zeros(p_ref[1][mp].shape, BF16)
        al_ref[1][mp][...] = jnp.ones(al_ref[1][mp].shape, F32)
    scores(0, 0)

    def pair(t, _):
        softmax(0)
        accumulate(jnp.maximum(2 * t - 1, 0), 1)
        scores(2 * t + 1, 1)
        softmax(1)
        accumulate(2 * t, 0)
        scores(2 * t + 2, 0)
        return 0

    lax.fori_loop(0, qi, pair, 0)
    scores(2 * qi + 1, 1, diag=1)
    accumulate(jnp.maximum(2 * qi - 1, 0), 1)
    softmax(0, diag=0)
    accumulate(2 * qi, 0)
    softmax(1, diag=1)
    accumulate(2 * qi + 1, 1, diag=1)

    lam = (jnp.exp(jnp.sum(lq1_ref[...] * lk1_ref[...], axis=-1, keepdims=True))
           - jnp.exp(jnp.sum(lq2_ref[...] * lk2_ref[...], axis=-1, keepdims=True)) + lambda_init)
    inv1 = 1.0 / l_ref[0][...]
    inv2 = lam / l_ref[1][...]
    o = jnp.concatenate(
        [acc_ref[0][:, j * LANES:(j + 1) * LANES] * inv1 - acc_ref[1][:, j * LANES:(j + 1) * LANES] * inv2
         for j in range(2 * dh // LANES)], axis=1)
    o = _rms(o) * sg_ref[...] * (1.0 - lambda_init)
    o_ref[...] = o.astype(o_ref.dtype)


def _diff_attention(q, kv, lq1, lk1, lq2, lk2, subln_g, lambda_init):
    b, s, aw = q.shape
    dh = lq1.shape[-1]
    heads = aw // (2 * dh)
    tq = min(1024, s)
    tk = tq // 2
    vec = lambda a: a.reshape(1, -1)
    small = lambda w: pl.BlockSpec((1, w), lambda bi, hi, qi: (0, 0))
    return pl.pallas_call(
        functools.partial(_attn_kernel, tq, tk, dh, lambda_init),
        grid=(b, heads, s // tq),
        in_specs=[pl.BlockSpec((None, tq, 2 * dh), lambda bi, hi, qi: (bi, qi, hi)),
                  pl.BlockSpec((None, s, 2 * dh), lambda bi, hi, qi: (bi, 0, hi)),
                  pl.BlockSpec((None, s, 2 * dh), lambda bi, hi, qi: (bi, 0, heads + hi)),
                  small(dh), small(dh), small(dh), small(dh), small(2 * dh)],
        out_specs=pl.BlockSpec((None, tq, 2 * dh), lambda bi, hi, qi: (bi, qi, hi)),
        out_shape=jax.ShapeDtypeStruct((b, s, aw), BF16),
        scratch_shapes=([pltpu.VMEM((tq, tk), F32)] * 4 + [pltpu.VMEM((tq, tk), BF16)] * 4
                        + [pltpu.VMEM((tq, LANES), F32)] * 8 + [pltpu.VMEM((tq, 2 * dh), F32)] * 2),
        compiler_params=_params("parallel", "parallel", "arbitrary"),
        name="diff_attention",
    )(q, kv, kv, vec(lq1), vec(lk1), vec(lq2), vec(lk2), vec(subln_g))


def _router_kernel(h_ref, y_ref, pg_ref, pgate_ref, g_ref, sh_ref, sc_ref, wr_ref, br_ref,
                   h1_ref, np_ref, idx_ref, gate_ref, rank_ref, cnt_ref, carry_ref):
    i = pl.program_id(0)

    @pl.when(i == 0)
    def _():
        carry_ref[...] = jnp.zeros_like(carry_ref)

    h1 = h_ref[...] + pgate_ref[...] * (_rms(y_ref[...]) * pg_ref[...])
    h1_ref[...] = h1
    n = (_rms(h1) * g_ref[...]) * (1.0 + sc_ref[...]) + sh_ref[...]
    tm, d = n.shape
    n_hi = n.astype(BF16)
    n_hi32 = n_hi.astype(F32)
    n_lo = (n - n_hi32).astype(BF16)

    bits = pltpu.bitcast(n_hi32, jnp.uint32)
    half = d // 2
    np_ref[...] = (lax.shift_right_logical(bits[:, :half], jnp.uint32(16))
                   | (bits[:, half:] & jnp.uint32(0xFFFF0000)))

    wr = wr_ref[...]
    w_hi = wr.astype(BF16)
    w_lo = (wr - w_hi.astype(F32)).astype(BF16)
    dims = (((1,), (1,)), ((), ()))
    logits = (lax.dot_general(w_hi, n_hi, dims, preferred_element_type=F32)
              + lax.dot_general(w_lo, n_hi, dims, preferred_element_type=F32)
              + lax.dot_general(w_hi, n_lo, dims, preferred_element_type=F32)
              + br_ref[...])
    n_exp = logits.shape[0]
    eid = lax.broadcasted_iota(jnp.int32, (n_exp, tm), 0).astype(F32)

    vals, sels, ids = [], [], []
    rem = logits
    for _ in range(TOP_K):
        m = jnp.max(rem, axis=0, keepdims=True)
        first = jnp.min(jnp.where(rem == m, eid, float(n_exp)), axis=0, keepdims=True)
        sel = eid == first
        rem = jnp.where(sel, -jnp.inf, rem)
        vals.append(m)
        sels.append(sel)
        ids.append(first)

    exps = [jnp.exp(v - vals[0]) for v in vals]
    denom = exps[0]
    for e in exps[1:]:
        denom = denom + e

    chosen = sels[0]
    for sel in sels[1:]:
        chosen = jnp.logical_or(chosen, sel)
    chosen = jnp.where(chosen, 1.0, 0.0)
    before = (lax.broadcasted_iota(jnp.int32, (tm, tm), 0)
              < lax.broadcasted_iota(jnp.int32, (tm, tm), 1))
    upper = jnp.where(before, 1.0, 0.0).astype(BF16)
    rank_excl = jnp.dot(chosen.astype(BF16), upper, preferred_element_type=F32) + carry_ref[...]

    for k in range(TOP_K):
        idx_ref[k:k + 1, :] = ids[k].astype(jnp.int32)
        gate_ref[k:k + 1, :] = exps[k] / denom
        rank_ref[k:k + 1, :] = jnp.sum(jnp.where(sels[k], rank_excl, 0.0), axis=0,
                                       keepdims=True).astype(jnp.int32)

    total = carry_ref[...] + jnp.sum(chosen, axis=1, keepdims=True)
    carry_ref[...] = total
    cnt_ref[...] = jnp.broadcast_to(total, cnt_ref.shape).astype(jnp.int32)


def _router(h, y, post_g, post_gate, g, shift, scale, w_r, b_r, seq):
    n, d = h.shape
    n_exp = w_r.shape[1]
    tm = min(256, seq)
    per_b = seq // tm
    tok = lambda dt: jax.ShapeDtypeStruct((TOP_K, n), dt)
    tok_spec = pl.BlockSpec((TOP_K, tm), lambda i: (0, i))
    row_spec = pl.BlockSpec((tm, d), lambda i: (i, 0))
    vec_spec = pl.BlockSpec((1, d), lambda i: (0, 0))
    mod_spec = pl.BlockSpec((None, 1, d), lambda i: (i // per_b, 0, 0))
    return pl.pallas_call(
        _router_kernel,
        grid=(n // tm,),
        in_specs=[row_spec, row_spec, vec_spec, mod_spec, vec_spec, mod_spec, mod_spec,
                  pl.BlockSpec((n_exp, d), lambda i: (0, 0)),
                  pl.BlockSpec((n_exp, 1), lambda i: (0, 0))],
        out_specs=[row_spec, pl.BlockSpec((tm, d // 2), lambda i: (i, 0)), tok_spec, tok_spec, tok_spec,
                   pl.BlockSpec((n_exp, LANES), lambda i: (0, 0))],
        out_shape=[jax.ShapeDtypeStruct((n, d), F32), jax.ShapeDtypeStruct((n, d // 2), jnp.uint32),
                   tok(jnp.int32), tok(F32), tok(jnp.int32),
                   jax.ShapeDtypeStruct((n_exp, LANES), jnp.int32)],
        scratch_shapes=[pltpu.VMEM((n_exp, 1), F32)],
        compiler_params=_params("arbitrary"),
        name="moe_router",
    )(h, y, post_g.reshape(1, d), post_gate, g.reshape(1, d), shift, scale, w_r.T,
      b_r.reshape(n_exp, 1))


def _dispatch_kernel(n_tok, tm, dest_ref, ztile_ref, np_ref, xs_ref, zeros, sem, zsem):
    i = pl.program_id(0)

    @pl.when(i == 0)
    def _():
        zeros[...] = jnp.zeros_like(zeros)
        rows = zeros.shape[0]

        def fill(t):
            return pltpu.make_async_copy(zeros, xs_ref.at[pl.ds(t * rows, rows)], zsem)

        def start(t, _):
            @pl.when(ztile_ref[t] == 1)
            def _():
                fill(t).start()
            return 0

        def finish(t, _):
            @pl.when(ztile_ref[t] == 1)
            def _():
                fill(t).wait()
            return 0

        n_tiles = xs_ref.shape[0] // rows
        lax.fori_loop(0, n_tiles, start, 0)
        lax.fori_loop(0, n_tiles, finish, 0)

    def row_copy(r, k):
        d = dest_ref[k * n_tok + i * tm + r]
        return pltpu.make_async_copy(np_ref.at[pl.ds(r, 1)], xs_ref.at[pl.ds(d, 1)], sem)

    def issue(r, _):
        for k in range(TOP_K):
            row_copy(r, k).start()
        return 0

    lax.fori_loop(0, tm, issue, 0)

    def drain(r, _):
        for k in range(TOP_K):
            row_copy(r, k).wait()
        return 0

    lax.fori_loop(0, tm, drain, 0)


def _dispatch(n_packed, dest_flat, zero_tile):
    n, half = n_packed.shape
    tm = min(256, n)
    n_rows = zero_tile.shape[0] * MOE_TILE
    return pl.pallas_call(
        functools.partial(_dispatch_kernel, n, tm),
        grid_spec=pltpu.PrefetchScalarGridSpec(
            num_scalar_prefetch=2,
            grid=(n // tm,),
            in_specs=[pl.BlockSpec((tm, half), lambda i, dest, zt: (i, 0))],
            out_specs=pl.BlockSpec(memory_space=pl.ANY),
            scratch_shapes=[pltpu.VMEM((MOE_TILE, half), jnp.uint32),
                            pltpu.SemaphoreType.DMA(()), pltpu.SemaphoreType.DMA(())]),
        out_shape=jax.ShapeDtypeStruct((n_rows, half), jnp.uint32),
        compiler_params=_params("arbitrary"),
        name="moe_dispatch",
    )(dest_flat, zero_tile, n_packed)


def _w1_prep_kernel(group, w_ref, p_ref, o_ref):
    for cb in range(w_ref.shape[1] // group):
        cols = slice(cb * group, (cb + 1) * group)
        o_ref[:, cols] = jnp.dot(w_ref[:, cols].astype(BF16), p_ref[...],
                                 preferred_element_type=F32).astype(BF16)


def _w1_prep(w1):
    shape = w1.shape
    cols = shape[-1]
    rows = math.prod(shape[:-1])
    group = min(2 * LANES, cols)
    tm = min(2048, rows)
    src = lax.broadcasted_iota(jnp.int32, (group, group), 0)
    dst = lax.broadcasted_iota(jnp.int32, (group, group), 1)
    perm = (dst == src // 2 + (group // 2) * (src % 2)).astype(BF16)
    out = pl.pallas_call(
        functools.partial(_w1_prep_kernel, group),
        grid=(rows // tm,),
        in_specs=[pl.BlockSpec((tm, cols), lambda i: (i, 0)),
                  pl.BlockSpec((group, group), lambda i: (0, 0))],
        out_specs=pl.BlockSpec((tm, cols), lambda i: (i, 0)),
        out_shape=jax.ShapeDtypeStruct((rows, cols), BF16),
        compiler_params=_params("parallel"),
        name="moe_w1_prep",
    )(w1.reshape(rows, cols), perm)
    return out.reshape(shape)


def _deinterleave_blocks(h, group):
    half = group // 2
    blocks = range(h.shape[1] // group)
    glu = jnp.concatenate([h[:, b * group:b * group + half] for b in blocks], axis=1)
    lin = jnp.concatenate([h[:, b * group + half:(b + 1) * group] for b in blocks], axis=1)
    return glu, lin


def _expert_kernel(group, texp_ref, nused_ref, first_ref, x_ref, w1_ref, w2_ref, b1g_ref, b1l_ref, b2_ref,
                   o_ref, w2b_ref):
    i = pl.program_id(0)

    @pl.when(first_ref[i] == 1)
    def _():
        w2b_ref[...] = w2_ref[...].astype(BF16)

    @pl.when(i < nused_ref[0])
    def _():
        w = x_ref[...]
        half = w.shape[1]
        x_lo = pltpu.bitcast(lax.shift_left(w, jnp.uint32(16)), F32).astype(BF16)
        x_hi = pltpu.bitcast(w & jnp.uint32(0xFFFF0000), F32).astype(BF16)
        hcat = (jnp.dot(x_lo, w1_ref[:half, :], preferred_element_type=F32)
                + jnp.dot(x_hi, w1_ref[half:, :], preferred_element_type=F32))
        glu, lin = _deinterleave_blocks(hcat, group)
        glu = jnp.minimum(glu + b1g_ref[...], SWIGLU_LIMIT)
        lin = jnp.clip(lin + b1l_ref[...], -SWIGLU_LIMIT, SWIGLU_LIMIT)
        act = glu * jax.nn.sigmoid(SWIGLU_ALPHA * glu) * (lin + 1.0)
        o_ref[...] = jnp.dot(act.astype(BF16), w2b_ref[...], preferred_element_type=F32) + b2_ref[...]

    @pl.when(i >= nused_ref[0])
    def _():
        o_ref[...] = jnp.zeros_like(o_ref)


def _experts(xs, tile_exp, n_used, first_tile, w1p, w2, layer, b1g, b1l, b2):
    n_rows, half = xs.shape
    _, n_exp, d, ff2 = w1p.shape
    ff = ff2 // 2
    n_tiles = n_rows // MOE_TILE
    tile = lambda i, te, nu, ft: (jnp.minimum(i, nu[0] - 1), 0)
    wmap = lambda i, te, nu, ft: (te[jnp.minimum(i, nu[0] - 1)], 0, 0)
    lwmap = lambda i, te, nu, ft: (layer, te[jnp.minimum(i, nu[0] - 1)], 0, 0)
    return pl.pallas_call(
        functools.partial(_expert_kernel, min(2 * LANES, ff2)),
        grid_spec=pltpu.PrefetchScalarGridSpec(
            num_scalar_prefetch=3,
            grid=(n_tiles,),
            in_specs=[pl.BlockSpec((MOE_TILE, half), tile),
                      pl.BlockSpec((None, None, d, ff2), lwmap),
                      pl.BlockSpec((None, None, ff, d), lwmap),
                      pl.BlockSpec((None, 1, ff), wmap),
                      pl.BlockSpec((None, 1, ff), wmap),
                      pl.BlockSpec((None, 1, d), wmap)],
            out_specs=pl.BlockSpec((MOE_TILE, d), lambda i, te, nu, ft: (i, 0)),
            scratch_shapes=[pltpu.VMEM((ff, d), BF16)]),
        out_shape=jax.ShapeDtypeStruct((n_rows, d), F32),
        compiler_params=_params("arbitrary"),
        name="moe_experts",
    )(tile_exp, n_used, first_tile, xs, w1p, w2, b1g, b1l, b2)


COMBINE_SLOTS = 3
COMBINE_ROW_CHUNK = 128


def _combine_kernel(n_tok, tm, n_mods, dest_ref, yb_ref, gt_ref, h_ref, g_ref, gate_ref, *refs):
    mod_refs = refs[:3 * n_mods]
    o_ref = refs[3 * n_mods]
    n_refs = refs[3 * n_mods + 1:4 * n_mods + 1]
    buf, sems = refs[4 * n_mods + 1:]
    i = pl.program_id(0)
    steps = pl.num_programs(0)

    def row_copy(step, slot, r, k):
        d = dest_ref[k * n_tok + step * tm + r]
        return pltpu.make_async_copy(yb_ref.at[pl.ds(d, 1)], buf.at[slot, k, pl.ds(r, 1)], sems.at[slot])

    def issue_tile(step, slot):
        def body(r, _):
            for k in range(TOP_K):
                row_copy(step, slot, r, k).start()
            return 0
        lax.fori_loop(0, tm, body, 0)

    def wait_tile(step, slot):
        def body(r, _):
            for k in range(TOP_K):
                row_copy(step, slot, r, k).wait()
            return 0
        lax.fori_loop(0, tm, body, 0)

    last = steps - 1

    @pl.when(i == 0)
    def _():
        issue_tile(0, 0)
        issue_tile(jnp.minimum(1, last), 1)

    slot = i % COMBINE_SLOTS
    wait_tile(i, slot)
    ahead = jnp.minimum(i + 2, last)
    ahead_slot = (i + 2) % COMBINE_SLOTS

    def rows_step(c, _):
        r0 = pl.multiple_of(c * COMBINE_ROW_CHUNK, COMBINE_ROW_CHUNK)
        rows = pl.ds(r0, COMBINE_ROW_CHUNK)
        y = buf[slot, 0, rows, :] * gt_ref[rows, 0:1]
        for k in range(1, TOP_K):
            y = y + buf[slot, k, rows, :] * gt_ref[rows, k:k + 1]
        h_new = h_ref[rows, :] + gate_ref[...] * (_rms(y) * g_ref[...])
        o_ref[rows, :] = h_new
        if n_mods:
            yn = _rms(h_new)
            for t in range(n_mods):
                mg_ref, sh_ref, sc_ref = mod_refs[3 * t:3 * t + 3]
                n_refs[t][rows, :] = ((yn * mg_ref[...]) * (1.0 + sc_ref[...]) + sh_ref[...]).astype(BF16)
        for rr in range(COMBINE_ROW_CHUNK):
            for k in range(TOP_K):
                row_copy(ahead, ahead_slot, r0 + rr, k).start()
        return 0

    lax.fori_loop(0, tm // COMBINE_ROW_CHUNK, rows_step, 0)

    @pl.when(i == last)
    def _():
        wait_tile(last, (i + 1) % COMBINE_SLOTS)
        wait_tile(last, (i + 2) % COMBINE_SLOTS)


def _combine(yb, dest_flat, gates_t, h, g, gate, seq, next_mods):
    n, d = h.shape
    tm = min(128, seq)
    per_b = seq // tm
    row_spec = pl.BlockSpec((tm, d), lambda i, dest: (i, 0))
    vec_spec = pl.BlockSpec((1, d), lambda i, dest: (0, 0))
    mod_spec = pl.BlockSpec((None, 1, d), lambda i, dest: (i // per_b, 0, 0))
    mod_ins, mod_specs = [], []
    for mg, sh, sc in next_mods:
        mod_ins += [mg.reshape(1, d), sh, sc]
        mod_specs += [vec_spec, mod_spec, mod_spec]
    outs = pl.pallas_call(
        functools.partial(_combine_kernel, n, tm, len(next_mods)),
        grid_spec=pltpu.PrefetchScalarGridSpec(
            num_scalar_prefetch=1,
            grid=(n // tm,),
            in_specs=[pl.BlockSpec(memory_space=pl.ANY),
                      pl.BlockSpec((tm, TOP_K), lambda i, dest: (i, 0)),
                      row_spec, vec_spec, mod_spec] + mod_specs,
            out_specs=[row_spec] * (1 + len(next_mods)),
            scratch_shapes=[pltpu.VMEM((COMBINE_SLOTS, TOP_K, tm, d), F32),
                            pltpu.SemaphoreType.DMA((COMBINE_SLOTS,))]),
        out_shape=[jax.ShapeDtypeStruct((n, d), F32)]
        + [jax.ShapeDtypeStruct((n, d), BF16) for _ in next_mods],
        compiler_params=_params("arbitrary"),
        name="moe_combine",
    )(dest_flat, yb, gates_t, h, g.reshape(1, d), gate, *mod_ins)
    return outs[0], list(outs[1:])


def _moe_layer(h, mix_out, mix_post_g, mix_gate, seq, pre_g, post_g, shift, scale, gate, w_r, b_r, w1p,
               layer, b1, w2, b2, next_mods):
    n, d = h.shape
    n_exp = w_r.shape[1]
    h, n_packed, idx, gates, rank, cnt = _router(h, mix_out, mix_post_g, mix_gate, pre_g, shift, scale,
                                                 w_r, b_r, seq)

    counts = cnt[:, 0]
    tiles_e = (counts + MOE_TILE - 1) // MOE_TILE
    tile_end = jnp.cumsum(tiles_e)
    row_start = (tile_end - tiles_e) * MOE_TILE
    experts = jnp.arange(n_exp, dtype=jnp.int32)[:, None, None]
    dest = jnp.sum(jnp.where(idx[None] == experts, row_start[:, None, None], 0), axis=0) + rank
    dest = dest.reshape(-1).astype(jnp.int32)
    n_tiles = (n * TOP_K) // MOE_TILE + n_exp
    tile_exp = jnp.sum(tile_end[None, :] <= jnp.arange(n_tiles)[:, None], axis=1)
    tile_exp = jnp.minimum(tile_exp, n_exp - 1).astype(jnp.int32)
    n_used = tile_end[-1:].astype(jnp.int32)
    tile_ids = jnp.arange(n_tiles)
    last_of_expert = jnp.any((tile_ids[:, None] == tile_end[None, :] - 1) & (tiles_e[None, :] > 0), axis=1)
    zero_tile = (last_of_expert | (tile_ids >= n_used[0])).astype(jnp.int32)
    first_tile = jnp.any((tile_ids[:, None] == (tile_end - tiles_e)[None, :]) & (tiles_e[None, :] > 0),
                         axis=1).astype(jnp.int32)

    xs = _dispatch(n_packed, dest, zero_tile)
    ff = w2.shape[2]
    yb = _experts(xs, tile_exp, n_used, first_tile, w1p, w2, layer,
                  b1[:, 0::2].reshape(n_exp, 1, ff), b1[:, 1::2].reshape(n_exp, 1, ff),
                  b2.reshape(n_exp, 1, d))
    return _combine(yb, dest, gates.T, h, post_g, gate, seq, next_mods)


def _lambda_init(layer):
    return 0.8 - 0.6 * math.exp(-0.3 * layer)


def kernel(x, c, mix_pre_g, mix_post_g, mix_mod_w, mix_mod_b, ffn_pre_g, ffn_post_g, ffn_mod_w, ffn_mod_b, a_w_in, a_ln_g, a_ln_b, a_w_s, a_b_s, a_w_out, kv_g, kv_mod_w, kv_mod_b, w_k, w_v, b_w_q, b_lq1, b_lk1, b_lq2, b_lk2, b_subln_g, b_w_o, moe_w_r, moe_b_r, moe_w1, moe_b1, moe_w2, moe_b2):
    batch, seq, d = x.shape
    depth = mix_pre_g.shape[0]
    n_a = a_w_in.shape[0]
    dh = b_lq1.shape[-1]
    h = x.reshape(batch * seq, d)
    c_pad = jnp.zeros((16, d), F32).at[:batch].set(c)
    moe_w1p = _w1_prep(moe_w1)

    mix_mods = [_modulation(c_pad, mix_mod_w, mix_mod_b, l, 3, batch) for l in range(depth)]
    ffn_mods = [_modulation(c_pad, ffn_mod_w, ffn_mod_b, l, 3, batch) for l in range(depth)]
    if depth > n_a:
        kv_shift, kv_scale = _modulation(c_pad, kv_mod_w[None], kv_mod_b[None], 0, 2, batch)

    def mixer_pre_norms(l):
        own = (mix_pre_g[l], mix_mods[l][0], mix_mods[l][1])
        return [(kv_g, kv_shift, kv_scale), own] if l == n_a else [own]

    norms = _modnorm(h, mixer_pre_norms(0), seq)
    kv = None
    for l in range(depth):
        n = norms[-1]
        if l < n_a:
            z = _matmul(n, a_w_in[l].astype(BF16), BF16, epilogue="gelu", name="gmlp_in")
            gated = _gmlp_gate(z, a_ln_g[l], a_ln_b[l], a_w_s[l], a_b_s[l])
            out = _matmul(gated, a_w_out[l].astype(BF16), F32, name="gmlp_out")
        else:
            j = l - n_a
            if l == n_a:
                w_kv = jnp.concatenate([w_k, w_v], axis=1).astype(BF16)
                kv = _matmul(norms[0], w_kv, BF16, name="kv_proj").reshape(batch, seq, -1)
            q = _matmul(n, b_w_q[j].astype(BF16), BF16, epilogue="scale",
                        scale=LOG2E * dh ** -0.5, name="q_proj").reshape(batch, seq, -1)
            o = _diff_attention(q, kv, b_lq1[j], b_lk1[j], b_lq2[j], b_lk2[j], b_subln_g[j],
                                _lambda_init(l))
            out = _matmul(o.reshape(batch * seq, -1), b_w_o[j].astype(BF16), F32, name="attn_out")

        shift, scale, gate = ffn_mods[l]
        next_mods = mixer_pre_norms(l + 1) if l + 1 < depth else []
        h, norms = _moe_layer(h, out, mix_post_g[l], mix_mods[l][2], seq, ffn_pre_g[l], ffn_post_g[l],
                              shift, scale, gate, moe_w_r[l], moe_b_r[l], moe_w1p, l, moe_b1[l], moe_w2,
                              moe_b2[l], next_mods)
    return h.reshape(batch, seq, d)
```

```python
import functools
import math

import jax
import jax.numpy as jnp
from jax import lax
from jax.experimental import pallas as pl
from jax.experimental.pallas import tpu as pltpu

RMS_EPS = 1e-6
LN_EPS = 1e-5
TOP_K = 4
SWIGLU_ALPHA = 1.702
SWIGLU_LIMIT = 7.0
LOG2E = 1.4426950408889634

VMEM_LIMIT_BYTES = 56 * 1024 * 1024
LANES = 128
MOE_TILE = 256

BF16 = jnp.bfloat16
F32 = jnp.float32


def _params(*sem):
    return pltpu.CompilerParams(dimension_semantics=sem, vmem_limit_bytes=VMEM_LIMIT_BYTES)


def _rms(x):
    return x * lax.rsqrt(jnp.mean(x * x, axis=-1, keepdims=True) + RMS_EPS)


def _mod_kernel(c_ref, w_ref, b_ref, o_ref):
    c = c_ref[...]
    s = c * jax.nn.sigmoid(c)
    s_hi = s.astype(BF16)
    s_lo = (s - s_hi.astype(F32)).astype(BF16)
    lhs = jnp.concatenate([s_hi, s_lo], axis=0)
    r = jnp.dot(lhs, w_ref[...].astype(BF16), preferred_element_type=F32)
    rows = c.shape[0]
    o_ref[...] = r[:rows] + r[rows:] + b_ref[...]


def _modulation(c_pad, w3, b2, layer, n_split, batch):
    rows, d = c_pad.shape
    nout = w3.shape[-1]
    tn = min(512, nout)
    out = pl.pallas_call(
        _mod_kernel,
        grid=(nout // tn,),
        in_specs=[
            pl.BlockSpec((rows, d), lambda j: (0, 0)),
            pl.BlockSpec((None, d, tn), lambda j: (layer, 0, j)),
            pl.BlockSpec((None, 1, tn), lambda j: (layer, 0, j)),
        ],
        out_specs=pl.BlockSpec((rows, tn), lambda j: (0, j)),
        out_shape=jax.ShapeDtypeStruct((rows, nout), F32),
        compiler_params=_params("arbitrary"),
        name="modulation",
    )(c_pad, w3, b2.reshape(b2.shape[0], 1, nout))
    return [t[:batch, None, :] for t in jnp.split(out, n_split, axis=-1)]


def _modnorm_kernel(n_out, h_ref, *refs):
    y = _rms(h_ref[...])
    for t in range(n_out):
        g_ref, sh_ref, sc_ref = refs[3 * t:3 * t + 3]
        o_ref = refs[3 * n_out + t]
        o_ref[...] = ((y * g_ref[...]) * (1.0 + sc_ref[...]) + sh_ref[...]).astype(o_ref.dtype)


def _modnorm(h, mods, seq):
    n, d = h.shape
    tm = min(512, seq)
    per_b = seq // tm
    ins, specs = [h], [pl.BlockSpec((tm, d), lambda i: (i, 0))]
    for g, sh, sc in mods:
        ins += [g.reshape(1, d), sh, sc]
        specs += [pl.BlockSpec((1, d), lambda i: (0, 0)),
                  pl.BlockSpec((None, 1, d), lambda i: (i // per_b, 0, 0)),
                  pl.BlockSpec((None, 1, d), lambda i: (i // per_b, 0, 0))]
    outs = pl.pallas_call(
        functools.partial(_modnorm_kernel, len(mods)),
        grid=(n // tm,),
        in_specs=specs,
        out_specs=[pl.BlockSpec((tm, d), lambda i: (i, 0)) for _ in mods],
        out_shape=[jax.ShapeDtypeStruct((n, d), BF16) for _ in mods],
        compiler_params=_params("parallel"),
        name="modnorm",
    )(*ins)
    return list(outs)


def _matmul_kernel(epilogue, scale, x_ref, w_ref, o_ref):
    acc = jnp.dot(x_ref[...], w_ref[...], preferred_element_type=F32)
    if epilogue == "gelu":
        acc = 0.5 * acc * (1.0 + lax.erf(acc * (1.0 / math.sqrt(2.0))))
    elif epilogue == "scale":
        acc = acc * scale
    o_ref[...] = acc.astype(o_ref.dtype)


def _matmul(x, w, out_dtype, epilogue=None, scale=1.0, name="matmul"):
    m, k = x.shape
    nn = w.shape[1]
    tm = min(1024, m)
    tn = min(1024 if out_dtype == BF16 else 512, nn)
    return pl.pallas_call(
        functools.partial(_matmul_kernel, epilogue, scale),
        grid=(m // tm, nn // tn),
        in_specs=[pl.BlockSpec((tm, k), lambda i, j: (i, 0)),
                  pl.BlockSpec((k, tn), lambda i, j: (0, j))],
        out_specs=pl.BlockSpec((tm, tn), lambda i, j: (i, j)),
        out_shape=jax.ShapeDtypeStruct((m, nn), out_dtype),
        compiler_params=_params("parallel", "arbitrary"),
        name=name,
    )(x, w)


def _gate_kernel(chunk, groups, u_ref, v_ref, g_ref, b_ref, ws_ref, bs_ref, o_ref):
    v = v_ref[...].astype(F32)
    mu = jnp.mean(v, axis=-1, keepdims=True)
    vc = v - mu
    var = jnp.mean(vc * vc, axis=-1, keepdims=True)
    vln = (vc * lax.rsqrt(var + LN_EPS) * g_ref[...] + b_ref[...]).astype(BF16)
    tm, width = vln.shape
    gd = width // groups
    causal = (lax.broadcasted_iota(jnp.int32, (chunk, chunk), 0)
              >= lax.broadcasted_iota(jnp.int32, (chunk, chunk), 1))
    for g in range(groups):
        ws = jnp.where(causal, ws_ref[g], 0.0).astype(BF16)
        bias = bs_ref[:, g:g + 1]
        for c in range(tm // chunk):
            rows = slice(c * chunk, (c + 1) * chunk)
            cols = slice(g * gd, (g + 1) * gd)
            sv = jnp.dot(ws, vln[rows, cols], preferred_element_type=F32) + bias
            o_ref[rows, cols] = (u_ref[rows, cols].astype(F32) * sv).astype(o_ref.dtype)


def _gmlp_gate(z, ln_g, ln_b, w_s, b_s):
    n, w2 = z.shape
    width = w2 // 2
    groups, chunk, _ = w_s.shape
    tm = min(2 * chunk, n)
    return pl.pallas_call(
        functools.partial(_gate_kernel, chunk, groups),
        grid=(n // tm,),
        in_specs=[pl.BlockSpec((tm, width), lambda i: (i, 0)),
                  pl.BlockSpec((tm, width), lambda i: (i, 1)),
                  pl.BlockSpec((1, width), lambda i: (0, 0)),
                  pl.BlockSpec((1, width), lambda i: (0, 0)),
                  pl.BlockSpec((groups, chunk, chunk), lambda i: (0, 0, 0)),
                  pl.BlockSpec((chunk, groups), lambda i: (0, 0))],
        out_specs=pl.BlockSpec((tm, width), lambda i: (i, 0)),
        out_shape=jax.ShapeDtypeStruct((n, width), BF16),
        compiler_params=_params("parallel"),
        name="gmlp_gate",
    )(z, z, ln_g.reshape(1, width), ln_b.reshape(1, width), w_s, b_s.T)


ATTN_ROW_CHUNK = 256


def _attn_kernel(tq, tk, dh, lambda_init, q_ref, k_ref, v_ref, lq1_ref, lk1_ref, lq2_ref, lk2_ref,
                 sg_ref, o_ref, *scratch):
    s_ref = (scratch[0:2], scratch[2:4])
    p_ref = (scratch[4:6], scratch[6:8])
    al_ref = (scratch[8:10], scratch[10:12])
    m_ref, l_ref, acc_ref = scratch[12:14], scratch[14:16], scratch[16:18]
    qi = pl.program_id(2)
    dims = (((1,), (1,)), ((), ()))
    rc = min(ATTN_ROW_CHUNK, tq)

    def live_rows(diag):
        return slice(tk, tq) if diag == 1 else slice(0, tq)

    def scores(blk, buf, diag=None):
        start = pl.multiple_of(blk * tk, tk)
        rows = live_rows(diag)
        for mp in range(2):
            s_ref[buf][mp][rows, :] = lax.dot_general(
                q_ref[rows, mp * dh:(mp + 1) * dh], k_ref[pl.ds(start, tk), pl.ds(mp * dh, dh)], dims,
                preferred_element_type=F32)

    def accumulate(blk, buf, diag=None):
        start = pl.multiple_of(blk * tk, tk)
        rows = live_rows(diag)
        v = v_ref[pl.ds(start, tk), :]
        for mp in range(2):
            alpha = jnp.tile(al_ref[buf][mp][rows, :], (1, 2 * dh // LANES))
            acc_ref[mp][rows, :] = alpha * acc_ref[mp][rows, :] + jnp.dot(
                p_ref[buf][mp][rows, :], v, preferred_element_type=F32)

    def softmax(buf, diag=None):
        nblk = tk // LANES
        first = 0 if diag != 1 else tk // rc
        for mp in range(2):
            for c in range(first, tq // rc):
                rows = slice(c * rc, (c + 1) * rc)
                sb = []
                for j in range(nblk):
                    col_lo = (diag or 0) * tk + j * LANES
                    if diag is not None and col_lo > c * rc + rc - 1:
                        sb.append(None)
                        continue
                    s = s_ref[buf][mp][rows, j * LANES:(j + 1) * LANES]
                    if diag is not None and col_lo + LANES - 1 > c * rc:
                        keep = (lax.broadcasted_iota(jnp.int32, (rc, LANES), 1) + (col_lo - c * rc)
                                <= lax.broadcasted_iota(jnp.int32, (rc, LANES), 0))
                        s = jnp.where(keep, s, -jnp.inf)
                    sb.append(s)
                live = [s for s in sb if s is not None]
                part = live[0]
                for s in live[1:]:
                    part = jnp.maximum(part, s)
                m_old = m_ref[mp][rows, :]
                m_new = jnp.maximum(m_old, jnp.max(part, axis=-1, keepdims=True))
                pb = [None if s is None else jnp.exp2(s - m_new) for s in sb]
                live = [p for p in pb if p is not None]
                part = live[0]
                for p in live[1:]:
                    part = part + p
                alpha = jnp.exp2(m_old - m_new)
                l_ref[mp][rows, :] = alpha * l_ref[mp][rows, :] + jnp.sum(part, axis=-1, keepdims=True)
                m_ref[mp][rows, :] = m_new
                al_ref[buf][mp][rows, :] = alpha
                for j in range(nblk):
                    p_ref[buf][mp][rows, j * LANES:(j + 1) * LANES] = (
                        jnp.zeros((rc, LANES), BF16) if pb[j] is None else pb[j].astype(BF16))

    for mp in range(2):
        m_ref[mp][...] = jnp.full(m_ref[mp].shape, -jnp.inf, F32)
        l_ref[mp][...] = jnp.zeros(l_ref[mp].shape, F32)
        acc_ref[mp][...] = jnp.zeros(acc_ref[mp].shape, F32)
        p_ref[1][mp][...] = jnp.zeros(p_ref[1][mp].shape, BF16)
        al_ref[1][mp][...] = jnp.ones(al_ref[1][mp].shape, F32)
    scores(0, 0)

    def pair(t, _):
        softmax(0)
        accumulate(jnp.maximum(2 * t - 1, 0), 1)
        scores(2 * t + 1, 1)
        softmax(1)
        accumulate(2 * t, 0)
        scores(2 * t + 2, 0)
        return 0

    lax.fori_loop(0, qi, pair, 0)
    scores(2 * qi + 1, 1, diag=1)
    accumulate(jnp.maximum(2 * qi - 1, 0), 1)
    softmax(0, diag=0)
    accumulate(2 * qi, 0)
    softmax(1, diag=1)
    accumulate(2 * qi + 1, 1, diag=1)

    lam = (jnp.exp(jnp.sum(lq1_ref[...] * lk1_ref[...], axis=-1, keepdims=True))
           - jnp.exp(jnp.sum(lq2_ref[...] * lk2_ref[...], axis=-1, keepdims=True)) + lambda_init)
    inv1 = 1.0 / l_ref[0][...]
    inv2 = lam / l_ref[1][...]
    o = jnp.concatenate(
        [acc_ref[0][:, j * LANES:(j + 1) * LANES] * inv1 - acc_ref[1][:, j * LANES:(j + 1) * LANES] * inv2
         for j in range(2 * dh // LANES)], axis=1)
    o = _rms(o) * sg_ref[...] * (1.0 - lambda_init)
    o_ref[...] = o.astype(o_ref.dtype)


def _diff_attention(q, kv, lq1, lk1, lq2, lk2, subln_g, lambda_init):
    b, s, aw = q.shape
    dh = lq1.shape[-1]
    heads = aw // (2 * dh)
    tq = min(1024, s)
    tk = tq // 2
    vec = lambda a: a.reshape(1, -1)
    small = lambda w: pl.BlockSpec((1, w), lambda bi, hi, qi: (0, 0))
    return pl.pallas_call(
        functools.partial(_attn_kernel, tq, tk, dh, lambda_init),
        grid=(b, heads, s // tq),
        in_specs=[pl.BlockSpec((None, tq, 2 * dh), lambda bi, hi, qi: (bi, qi, hi)),
                  pl.BlockSpec((None, s, 2 * dh), lambda bi, hi, qi: (bi, 0, hi)),
                  pl.BlockSpec((None, s, 2 * dh), lambda bi, hi, qi: (bi, 0, heads + hi)),
                  small(dh), small(dh), small(dh), small(dh), small(2 * dh)],
        out_specs=pl.BlockSpec((None, tq, 2 * dh), lambda bi, hi, qi: (bi, qi, hi)),
        out_shape=jax.ShapeDtypeStruct((b, s, aw), BF16),
        scratch_shapes=([pltpu.VMEM((tq, tk), F32)] * 4 + [pltpu.VMEM((tq, tk), BF16)] * 4
                        + [pltpu.VMEM((tq, LANES), F32)] * 8 + [pltpu.VMEM((tq, 2 * dh), F32)] * 2),
        compiler_params=_params("parallel", "parallel", "arbitrary"),
        name="diff_attention",
    )(q, kv, kv, vec(lq1), vec(lk1), vec(lq2), vec(lk2), vec(subln_g))


def _router_kernel(h_ref, y_ref, pg_ref, pgate_ref, g_ref, sh_ref, sc_ref, wr_ref, br_ref,
                   h1_ref, np_ref, idx_ref, gate_ref, rank_ref, cnt_ref, carry_ref):
    i = pl.program_id(0)

    @pl.when(i == 0)
    def _():
        carry_ref[...] = jnp.zeros_like(carry_ref)

    h1 = h_ref[...] + pgate_ref[...] * (_rms(y_ref[...]) * pg_ref[...])
    h1_ref[...] = h1
    n = (_rms(h1) * g_ref[...]) * (1.0 + sc_ref[...]) + sh_ref[...]
    tm, d = n.shape
    n_hi = n.astype(BF16)
    n_hi32 = n_hi.astype(F32)
    n_lo = (n - n_hi32).astype(BF16)

    bits = pltpu.bitcast(n_hi32, jnp.uint32)
    half = d // 2
    np_ref[...] = (lax.shift_right_logical(bits[:, :half], jnp.uint32(16))
                   | (bits[:, half:] & jnp.uint32(0xFFFF0000)))

    wr = wr_ref[...]
    w_hi = wr.astype(BF16)
    w_lo = (wr - w_hi.astype(F32)).astype(BF16)
    dims = (((1,), (1,)), ((), ()))
    logits = (lax.dot_general(w_hi, n_hi, dims, preferred_element_type=F32)
              + lax.dot_general(w_lo, n_hi, dims, preferred_element_type=F32)
              + lax.dot_general(w_hi, n_lo, dims, preferred_element_type=F32)
              + br_ref[...])
    n_exp = logits.shape[0]
    eid = lax.broadcasted_iota(jnp.int32, (n_exp, tm), 0).astype(F32)

    vals, sels, ids = [], [], []
    rem = logits
    for _ in range(TOP_K):
        m = jnp.max(rem, axis=0, keepdims=True)
        first = jnp.min(jnp.where(rem == m, eid, float(n_exp)), axis=0, keepdims=True)
        sel = eid == first
        rem = jnp.where(sel, -jnp.inf, rem)
        vals.append(m)
        sels.append(sel)
        ids.append(first)

    exps = [jnp.exp(v - vals[0]) for v in vals]
    denom = exps[0]
    for e in exps[1:]:
        denom = denom + e

    chosen = sels[0]
    for sel in sels[1:]:
        chosen = jnp.logical_or(chosen, sel)
    chosen = jnp.where(chosen, 1.0, 0.0)
    before = (lax.broadcasted_iota(jnp.int32, (tm, tm), 0)
              < lax.broadcasted_iota(jnp.int32, (tm, tm), 1))
    upper = jnp.where(before, 1.0, 0.0).astype(BF16)
    rank_excl = jnp.dot(chosen.astype(BF16), upper, preferred_element_type=F32) + carry_ref[...]

    for k in range(TOP_K):
        idx_ref[k:k + 1, :] = ids[k].astype(jnp.int32)
        gate_ref[k:k + 1, :] = exps[k] / denom
        rank_ref[k:k + 1, :] = jnp.sum(jnp.where(sels[k], rank_excl, 0.0), axis=0,
                                       keepdims=True).astype(jnp.int32)

    total = carry_ref[...] + jnp.sum(chosen, axis=1, keepdims=True)
    carry_ref[...] = total
    cnt_ref[...] = jnp.broadcast_to(total, cnt_ref.shape).astype(jnp.int32)


def _router(h, y, post_g, post_gate, g, shift, scale, w_r, b_r, seq):
    n, d = h.shape
    n_exp = w_r.shape[1]
    tm = min(256, seq)
    per_b = seq // tm
    tok = lambda dt: jax.ShapeDtypeStruct((TOP_K, n), dt)
    tok_spec = pl.BlockSpec((TOP_K, tm), lambda i: (0, i))
    row_spec = pl.BlockSpec((tm, d), lambda i: (i, 0))
    vec_spec = pl.BlockSpec((1, d), lambda i: (0, 0))
    mod_spec = pl.BlockSpec((None, 1, d), lambda i: (i // per_b, 0, 0))
    return pl.pallas_call(
        _router_kernel,
        grid=(n // tm,),
        in_specs=[row_spec, row_spec, vec_spec, mod_spec, vec_spec, mod_spec, mod_spec,
                  pl.BlockSpec((n_exp, d), lambda i: (0, 0)),
                  pl.BlockSpec((n_exp, 1), lambda i: (0, 0))],
        out_specs=[row_spec, pl.BlockSpec((tm, d // 2), lambda i: (i, 0)), tok_spec, tok_spec, tok_spec,
                   pl.BlockSpec((n_exp, LANES), lambda i: (0, 0))],
        out_shape=[jax.ShapeDtypeStruct((n, d), F32), jax.ShapeDtypeStruct((n, d // 2), jnp.uint32),
                   tok(jnp.int32), tok(F32), tok(jnp.int32),
                   jax.ShapeDtypeStruct((n_exp, LANES), jnp.int32)],
        scratch_shapes=[pltpu.VMEM((n_exp, 1), F32)],
        compiler_params=_params("arbitrary"),
        name="moe_router",
    )(h, y, post_g.reshape(1, d), post_gate, g.reshape(1, d), shift, scale, w_r.T,
      b_r.reshape(n_exp, 1))


def _dispatch_kernel(n_tok, tm, dest_ref, ztile_ref, np_ref, xs_ref, zeros, sem, zsem):
    i = pl.program_id(0)

    @pl.when(i == 0)
    def _():
        zeros[...] = jnp.zeros_like(zeros)
        rows = zeros.shape[0]

        def fill(t):
            return pltpu.make_async_copy(zeros, xs_ref.at[pl.ds(t * rows, rows)], zsem)

        def start(t, _):
            @pl.when(ztile_ref[t] == 1)
            def _():
                fill(t).start()
            return 0

        def finish(t, _):
            @pl.when(ztile_ref[t] == 1)
            def _():
                fill(t).wait()
            return 0

        n_tiles = xs_ref.shape[0] // rows
        lax.fori_loop(0, n_tiles, start, 0)
        lax.fori_loop(0, n_tiles, finish, 0)

    def row_copy(r, k):
        d = dest_ref[k * n_tok + i * tm + r]
        return pltpu.make_async_copy(np_ref.at[pl.ds(r, 1)], xs_ref.at[pl.ds(d, 1)], sem)

    def issue(r, _):
        for k in range(TOP_K):
            row_copy(r, k).start(priority=k % 2)
        return 0

    lax.fori_loop(0, tm, issue, 0)

    def drain(r, _):
        for k in range(TOP_K):
            row_copy(r, k).wait()
        return 0

    lax.fori_loop(0, tm, drain, 0)


def _dispatch(n_packed, dest_flat, zero_tile):
    n, half = n_packed.shape
    tm = min(256, n)
    n_rows = zero_tile.shape[0] * MOE_TILE
    return pl.pallas_call(
        functools.partial(_dispatch_kernel, n, tm),
        grid_spec=pltpu.PrefetchScalarGridSpec(
            num_scalar_prefetch=2,
            grid=(n // tm,),
            in_specs=[pl.BlockSpec((tm, half), lambda i, dest, zt: (i, 0))],
            out_specs=pl.BlockSpec(memory_space=pl.ANY),
            scratch_shapes=[pltpu.VMEM((MOE_TILE, half), jnp.uint32),
                            pltpu.SemaphoreType.DMA(()), pltpu.SemaphoreType.DMA(())]),
        out_shape=jax.ShapeDtypeStruct((n_rows, half), jnp.uint32),
        compiler_params=_params("arbitrary"),
        name="moe_dispatch",
    )(dest_flat, zero_tile, n_packed)


def _w1_prep_kernel(group, w_ref, p_ref, o_ref):
    for cb in range(w_ref.shape[1] // group):
        cols = slice(cb * group, (cb + 1) * group)
        o_ref[:, cols] = jnp.dot(w_ref[:, cols].astype(BF16), p_ref[...],
                                 preferred_element_type=F32).astype(BF16)


def _w1_prep(w1):
    shape = w1.shape
    cols = shape[-1]
    rows = math.prod(shape[:-1])
    group = min(2 * LANES, cols)
    tm = min(2048, rows)
    src = lax.broadcasted_iota(jnp.int32, (group, group), 0)
    dst = lax.broadcasted_iota(jnp.int32, (group, group), 1)
    perm = (dst == src // 2 + (group // 2) * (src % 2)).astype(BF16)
    out = pl.pallas_call(
        functools.partial(_w1_prep_kernel, group),
        grid=(rows // tm,),
        in_specs=[pl.BlockSpec((tm, cols), lambda i: (i, 0)),
                  pl.BlockSpec((group, group), lambda i: (0, 0))],
        out_specs=pl.BlockSpec((tm, cols), lambda i: (i, 0)),
        out_shape=jax.ShapeDtypeStruct((rows, cols), BF16),
        compiler_params=_params("parallel"),
        name="moe_w1_prep",
    )(w1.reshape(rows, cols), perm)
    return out.reshape(shape)


def _deinterleave_blocks(h, group):
    half = group // 2
    blocks = range(h.shape[1] // group)
    glu = jnp.concatenate([h[:, b * group:b * group + half] for b in blocks], axis=1)
    lin = jnp.concatenate([h[:, b * group + half:(b + 1) * group] for b in blocks], axis=1)
    return glu, lin


def _expert_kernel(group, texp_ref, nused_ref, first_ref, x_ref, w1_ref, w2_ref, b1g_ref, b1l_ref, b2_ref,
                   o_ref, w2b_ref):
    i = pl.program_id(0)

    @pl.when(first_ref[i] == 1)
    def _():
        w2b_ref[...] = w2_ref[...].astype(BF16)

    @pl.when(i < nused_ref[0])
    def _():
        w = x_ref[...]
        half = w.shape[1]
        x_lo = pltpu.bitcast(lax.shift_left(w, jnp.uint32(16)), F32).astype(BF16)
        x_hi = pltpu.bitcast(w & jnp.uint32(0xFFFF0000), F32).astype(BF16)
        hcat = (jnp.dot(x_lo, w1_ref[:half, :], preferred_element_type=F32)
                + jnp.dot(x_hi, w1_ref[half:, :], preferred_element_type=F32))
        glu, lin = _deinterleave_blocks(hcat, group)
        glu = jnp.minimum(glu + b1g_ref[...], SWIGLU_LIMIT)
        lin = jnp.clip(lin + b1l_ref[...], -SWIGLU_LIMIT, SWIGLU_LIMIT)
        act = glu * jax.nn.sigmoid(SWIGLU_ALPHA * glu) * (lin + 1.0)
        o_ref[...] = jnp.dot(act.astype(BF16), w2b_ref[...], preferred_element_type=F32) + b2_ref[...]

    @pl.when(i >= nused_ref[0])
    def _():
        o_ref[...] = jnp.zeros_like(o_ref)


def _experts(xs, tile_exp, n_used, first_tile, w1p, w2, layer, b1g, b1l, b2):
    n_rows, half = xs.shape
    _, n_exp, d, ff2 = w1p.shape
    ff = ff2 // 2
    n_tiles = n_rows // MOE_TILE
    tile = lambda i, te, nu, ft: (jnp.minimum(i, nu[0] - 1), 0)
    wmap = lambda i, te, nu, ft: (te[jnp.minimum(i, nu[0] - 1)], 0, 0)
    lwmap = lambda i, te, nu, ft: (layer, te[jnp.minimum(i, nu[0] - 1)], 0, 0)
    return pl.pallas_call(
        functools.partial(_expert_kernel, min(2 * LANES, ff2)),
        grid_spec=pltpu.PrefetchScalarGridSpec(
            num_scalar_prefetch=3,
            grid=(n_tiles,),
            in_specs=[pl.BlockSpec((MOE_TILE, half), tile),
                      pl.BlockSpec((None, None, d, ff2), lwmap),
                      pl.BlockSpec((None, None, ff, d), lwmap),
                      pl.BlockSpec((None, 1, ff), wmap),
                      pl.BlockSpec((None, 1, ff), wmap),
                      pl.BlockSpec((None, 1, d), wmap)],
            out_specs=pl.BlockSpec((MOE_TILE, d), lambda i, te, nu, ft: (i, 0)),
            scratch_shapes=[pltpu.VMEM((ff, d), BF16)]),
        out_shape=jax.ShapeDtypeStruct((n_rows, d), F32),
        compiler_params=_params("arbitrary"),
        name="moe_experts",
    )(tile_exp, n_used, first_tile, xs, w1p, w2, b1g, b1l, b2)


COMBINE_SLOTS = 3
COMBINE_ROW_CHUNK = 128


def _combine_kernel(n_tok, tm, n_mods, dest_ref, yb_ref, gt_ref, h_ref, g_ref, gate_ref, *refs):
    mod_refs = refs[:3 * n_mods]
    o_ref = refs[3 * n_mods]
    n_refs = refs[3 * n_mods + 1:4 * n_mods + 1]
    buf, sems = refs[4 * n_mods + 1:]
    i = pl.program_id(0)
    steps = pl.num_programs(0)

    def row_copy(step, slot, r, k):
        d = dest_ref[k * n_tok + step * tm + r]
        return pltpu.make_async_copy(yb_ref.at[pl.ds(d, 1)], buf.at[slot, k, pl.ds(r, 1)], sems.at[slot])

    def issue_tile(step, slot):
        def body(r, _):
            for k in range(TOP_K):
                row_copy(step, slot, r, k).start(priority=k % 2)
            return 0
        lax.fori_loop(0, tm, body, 0)

    def wait_tile(step, slot):
        def body(r, _):
            for k in range(TOP_K):
                row_copy(step, slot, r, k).wait()
            return 0
        lax.fori_loop(0, tm, body, 0)

    last = steps - 1

    @pl.when(i == 0)
    def _():
        issue_tile(0, 0)
        issue_tile(jnp.minimum(1, last), 1)

    slot = i % COMBINE_SLOTS
    wait_tile(i, slot)
    ahead = jnp.minimum(i + 2, last)
    ahead_slot = (i + 2) % COMBINE_SLOTS

    def rows_step(c, _):
        r0 = pl.multiple_of(c * COMBINE_ROW_CHUNK, COMBINE_ROW_CHUNK)
        rows = pl.ds(r0, COMBINE_ROW_CHUNK)
        y = buf[slot, 0, rows, :] * gt_ref[rows, 0:1]
        for k in range(1, TOP_K):
            y = y + buf[slot, k, rows, :] * gt_ref[rows, k:k + 1]
        h_new = h_ref[rows, :] + gate_ref[...] * (_rms(y) * g_ref[...])
        o_ref[rows, :] = h_new
        if n_mods:
            yn = _rms(h_new)
            for t in range(n_mods):
                mg_ref, sh_ref, sc_ref = mod_refs[3 * t:3 * t + 3]
                n_refs[t][rows, :] = ((yn * mg_ref[...]) * (1.0 + sc_ref[...]) + sh_ref[...]).astype(BF16)
        for rr in range(COMBINE_ROW_CHUNK):
            for k in range(TOP_K):
                row_copy(ahead, ahead_slot, r0 + rr, k).start(priority=k % 2)
        return 0

    lax.fori_loop(0, tm // COMBINE_ROW_CHUNK, rows_step, 0)

    @pl.when(i == last)
    def _():
        wait_tile(last, (i + 1) % COMBINE_SLOTS)
        wait_tile(last, (i + 2) % COMBINE_SLOTS)


def _combine(yb, dest_flat, gates_t, h, g, gate, seq, next_mods):
    n, d = h.shape
    tm = min(128, seq)
    per_b = seq // tm
    row_spec = pl.BlockSpec((tm, d), lambda i, dest: (i, 0))
    vec_spec = pl.BlockSpec((1, d), lambda i, dest: (0, 0))
    mod_spec = pl.BlockSpec((None, 1, d), lambda i, dest: (i // per_b, 0, 0))
    mod_ins, mod_specs = [], []
    for mg, sh, sc in next_mods:
        mod_ins += [mg.reshape(1, d), sh, sc]
        mod_specs += [vec_spec, mod_spec, mod_spec]
    outs = pl.pallas_call(
        functools.partial(_combine_kernel, n, tm, len(next_mods)),
        grid_spec=pltpu.PrefetchScalarGridSpec(
            num_scalar_prefetch=1,
            grid=(n // tm,),
            in_specs=[pl.BlockSpec(memory_space=pl.ANY),
                      pl.BlockSpec((tm, TOP_K), lambda i, dest: (i, 0)),
                      row_spec, vec_spec, mod_spec] + mod_specs,
            out_specs=[row_spec] * (1 + len(next_mods)),
            scratch_shapes=[pltpu.VMEM((COMBINE_SLOTS, TOP_K, tm, d), F32),
                            pltpu.SemaphoreType.DMA((COMBINE_SLOTS,))]),
        out_shape=[jax.ShapeDtypeStruct((n, d), F32)]
        + [jax.ShapeDtypeStruct((n, d), BF16) for _ in next_mods],
        compiler_params=_params("arbitrary"),
        name="moe_combine",
    )(dest_flat, yb, gates_t, h, g.reshape(1, d), gate, *mod_ins)
    return outs[0], list(outs[1:])


def _moe_layer(h, mix_out, mix_post_g, mix_gate, seq, pre_g, post_g, shift, scale, gate, w_r, b_r, w1p,
               layer, b1, w2, b2, next_mods):
    n, d = h.shape
    n_exp = w_r.shape[1]
    h, n_packed, idx, gates, rank, cnt = _router(h, mix_out, mix_post_g, mix_gate, pre_g, shift, scale,
                                                 w_r, b_r, seq)

    counts = cnt[:, 0]
    tiles_e = (counts + MOE_TILE - 1) // MOE_TILE
    tile_end = jnp.cumsum(tiles_e)
    row_start = (tile_end - tiles_e) * MOE_TILE
    experts = jnp.arange(n_exp, dtype=jnp.int32)[:, None, None]
    dest = jnp.sum(jnp.where(idx[None] == experts, row_start[:, None, None], 0), axis=0) + rank
    dest = dest.reshape(-1).astype(jnp.int32)
    n_tiles = (n * TOP_K) // MOE_TILE + n_exp
    tile_exp = jnp.sum(tile_end[None, :] <= jnp.arange(n_tiles)[:, None], axis=1)
    tile_exp = jnp.minimum(tile_exp, n_exp - 1).astype(jnp.int32)
    n_used = tile_end[-1:].astype(jnp.int32)
    tile_ids = jnp.arange(n_tiles)
    last_of_expert = jnp.any((tile_ids[:, None] == tile_end[None, :] - 1) & (tiles_e[None, :] > 0), axis=1)
    zero_tile = (last_of_expert | (tile_ids >= n_used[0])).astype(jnp.int32)
    first_tile = jnp.any((tile_ids[:, None] == (tile_end - tiles_e)[None, :]) & (tiles_e[None, :] > 0),
                         axis=1).astype(jnp.int32)

    xs = _dispatch(n_packed, dest, zero_tile)
    ff = w2.shape[2]
    yb = _experts(xs, tile_exp, n_used, first_tile, w1p, w2, layer,
                  b1[:, 0::2].reshape(n_exp, 1, ff), b1[:, 1::2].reshape(n_exp, 1, ff),
                  b2.reshape(n_exp, 1, d))
    return _combine(yb, dest, gates.T, h, post_g, gate, seq, next_mods)


def _lambda_init(layer):
    return 0.8 - 0.6 * math.exp(-0.3 * layer)


def kernel(x, c, mix_pre_g, mix_post_g, mix_mod_w, mix_mod_b, ffn_pre_g, ffn_post_g, ffn_mod_w, ffn_mod_b, a_w_in, a_ln_g, a_ln_b, a_w_s, a_b_s, a_w_out, kv_g, kv_mod_w, kv_mod_b, w_k, w_v, b_w_q, b_lq1, b_lk1, b_lq2, b_lk2, b_subln_g, b_w_o, moe_w_r, moe_b_r, moe_w1, moe_b1, moe_w2, moe_b2):
    batch, seq, d = x.shape
    depth = mix_pre_g.shape[0]
    n_a = a_w_in.shape[0]
    dh = b_lq1.shape[-1]
    h = x.reshape(batch * seq, d)
    c_pad = jnp.zeros((16, d), F32).at[:batch].set(c)
    moe_w1p = _w1_prep(moe_w1)

    mix_mods = [_modulation(c_pad, mix_mod_w, mix_mod_b, l, 3, batch) for l in range(depth)]
    ffn_mods = [_modulation(c_pad, ffn_mod_w, ffn_mod_b, l, 3, batch) for l in range(depth)]
    if depth > n_a:
        kv_shift, kv_scale = _modulation(c_pad, kv_mod_w[None], kv_mod_b[None], 0, 2, batch)

    def mixer_pre_norms(l):
        own = (mix_pre_g[l], mix_mods[l][0], mix_mods[l][1])
        return [(kv_g, kv_shift, kv_scale), own] if l == n_a else [own]

    norms = _modnorm(h, mixer_pre_norms(0), seq)
    kv = None
    for l in range(depth):
        n = norms[-1]
        if l < n_a:
            z = _matmul(n, a_w_in[l].astype(BF16), BF16, epilogue="gelu", name="gmlp_in")
            gated = _gmlp_gate(z, a_ln_g[l], a_ln_b[l], a_w_s[l], a_b_s[l])
            out = _matmul(gated, a_w_out[l].astype(BF16), F32, name="gmlp_out")
        else:
            j = l - n_a
            if l == n_a:
                w_kv = jnp.concatenate([w_k, w_v], axis=1).astype(BF16)
                kv = _matmul(norms[0], w_kv, BF16, name="kv_proj").reshape(batch, seq, -1)
            q = _matmul(n, b_w_q[j].astype(BF16), BF16, epilogue="scale",
                        scale=LOG2E * dh ** -0.5, name="q_proj").reshape(batch, seq, -1)
            o = _diff_attention(q, kv, b_lq1[j], b_lk1[j], b_lq2[j], b_lk2[j], b_subln_g[j],
                                _lambda_init(l))
            out = _matmul(o.reshape(batch * seq, -1), b_w_o[j].astype(BF16), F32, name="attn_out")

        shift, scale, gate = ffn_mods[l]
        next_mods = mixer_pre_norms(l + 1) if l + 1 < depth else []
        h, norms = _moe_layer(h, out, mix_post_g[l], mix_mods[l][2], seq, ffn_pre_g[l], ffn_post_g[l],
                              shift, scale, gate, moe_w_r[l], moe_b_r[l], moe_w1p, l, moe_b1[l], moe_w2,
                              moe_b2[l], next_mods)
    return h.reshape(batch, seq, d)
```
